```python
import math
import jax, jax.numpy as jnp
from jax import lax
import numpy as np

D_MODEL = 1024
BATCH = 4
SEQ = 8192
DEPTH = 1

EXPAND = 2
D_MIX = EXPAND * D_MODEL
RET_WIDTH = D_MIX // 2
MLA_WIDTH = D_MIX - RET_WIDTH
RET_V_DIM = 128
RET_QK_DIM = 64
RET_HEADS = RET_WIDTH // RET_V_DIM
MLA_V = 128
MLA_NOPE = 64
MLA_ROPE = 32
MLA_HEADS = MLA_WIDTH // MLA_V
Q_LORA = 384
KV_LORA = 256
CHUNK = 128
Q_BLOCK = 128
ROPE_BASE = 10000.0
EPS = 1e-6

SPLIT_SIZES = (
    RET_HEADS * RET_QK_DIM,
    RET_HEADS * RET_QK_DIM,
    RET_WIDTH,
    RET_WIDTH,
    Q_LORA,
    KV_LORA,
    MLA_ROPE,
    MLA_WIDTH,
)
D_IN = sum(SPLIT_SIZES)

kernel_name = "hybrid_retention_mla_block"


def rms_norm(x, w):
    xf = x.astype(jnp.float32)
    y = xf * lax.rsqrt(jnp.mean(xf * xf, axis=-1, keepdims=True) + EPS)
    return (y * w.astype(jnp.float32)).astype(x.dtype)


def rope(x, positions):
    d = x.shape[-1]
    inv_freq = ROPE_BASE ** (-jnp.arange(0, d, 2, dtype=jnp.float32) / d)
    ang = positions.astype(jnp.float32)[:, :, None] * inv_freq
    if x.ndim == 4:
        ang = ang[:, :, None, :]
    cos, sin = jnp.cos(ang), jnp.sin(ang)
    xf = x.astype(jnp.float32)
    x1, x2 = xf[..., : d // 2], xf[..., d // 2:]
    return jnp.concatenate([x1 * cos - x2 * sin, x2 * cos + x1 * sin], axis=-1).astype(x.dtype)


def split_cols(t, sizes):
    out, off = [], 0
    for s in sizes:
        out.append(t[..., off:off + s])
        off += s
    return out


def retention_dir(q, k, v, log_gamma, include_diag):
    B, H, S, dk = q.shape
    dv = v.shape[-1]
    n = S // CHUNK
    qc = q.reshape(B, H, n, CHUNK, dk)
    kc = k.reshape(B, H, n, CHUNK, dk)
    vc = v.reshape(B, H, n, CHUNK, dv)
    idx = jnp.arange(CHUNK, dtype=jnp.float32)
    diff = idx[:, None] - idx[None, :]
    mask = diff >= 0 if include_diag else diff > 0
    lg = log_gamma[:, None, None]
    decay_intra = jnp.where(mask[None], jnp.exp(lg * jnp.maximum(diff, 0.0)[None]), 0.0)
    scores = jnp.einsum('bhnik,bhnjk->bhnij', qc, kc) * decay_intra[None, :, None]
    intra = jnp.einsum('bhnij,bhnjv->bhniv', scores, vc)
    k_decay = jnp.exp(log_gamma[:, None] * (CHUNK - 1 - idx)[None])
    kv_chunk = jnp.einsum('bhnjk,bhnjv->bhnkv', kc * k_decay[None, :, None, :, None], vc)
    chunk_decay = jnp.exp(log_gamma * CHUNK)[None, :, None, None]

    def step(state, kv):
        return state * chunk_decay + kv, state

    init = jnp.zeros((B, H, dk, dv), jnp.float32)
    _, states = lax.scan(step, init, jnp.moveaxis(kv_chunk, 2, 0))
    states = jnp.moveaxis(states, 0, 2)
    q_decay = jnp.exp(log_gamma[:, None] * (idx + 1.0)[None])
    cross = jnp.einsum('bhnik,bhnkv->bhniv', qc * q_decay[None, :, None, :, None], states)
    return (intra + cross).reshape(B, H, S, dv)


def setup_inputs(seed: int = 0) -> dict:
    key = jax.random.key(seed)
    ks = jax.random.split(key, 24)
    f32 = jnp.float32

    def nrm(k, shape, scale):
        return jax.random.normal(k, shape, f32) * scale

    def gain(k, shape):
        return 1.0 + 0.02 * jax.random.normal(k, shape, f32)

    x = jax.random.normal(ks[0], (BATCH, SEQ, D_MODEL), f32)
    c = jax.random.normal(ks[1], (BATCH, D_MODEL), f32)
    offsets = jax.random.randint(ks[2], (BATCH, 1), 0, 4096, dtype=jnp.int32)
    positions = jnp.arange(SEQ, dtype=jnp.int32)[None, :] + offsets
    heads = np.arange(RET_HEADS, dtype=np.float32)
    base_logit = jnp.asarray(np.log(2.0 ** (5.0 + heads) - 1.0), f32)
    return {
        "x": x,
        "c": c,
        "positions": positions,
        "norm_w": gain(ks[3], (D_MODEL,)),
        "w_ada": nrm(ks[4], (D_MODEL, 3 * D_MODEL), 0.2 * D_MODEL ** -0.5),
        "b_ada": nrm(ks[5], (3 * D_MODEL,), 0.02),
        "w_in": nrm(ks[6], (D_MODEL, D_IN), D_MODEL ** -0.5),
        "ret_decay_logit_fwd": base_logit + 0.1 * jax.random.normal(ks[7], (RET_HEADS,), f32),
        "ret_decay_logit_bwd": base_logit + 0.1 * jax.random.normal(ks[8], (RET_HEADS,), f32),
        "ret_gn_w": gain(ks[9], (RET_HEADS, RET_V_DIM)),
        "q_norm_w": gain(ks[10], (Q_LORA,)),
        "w_uq": nrm(ks[11], (Q_LORA, MLA_HEADS * (MLA_NOPE + MLA_ROPE)), Q_LORA ** -0.5),
        "kv_norm_w": gain(ks[12], (KV_LORA,)),
        "w_ukv": nrm(ks[13], (KV_LORA, MLA_HEADS * (MLA_NOPE + MLA_V)), KV_LORA ** -0.5),
        "qn_nope_w": gain(ks[14], (MLA_NOPE,)),
        "qn_rope_w": gain(ks[15], (MLA_ROPE,)),
        "kn_nope_w": gain(ks[16], (MLA_NOPE,)),
        "kn_rope_w": gain(ks[17], (MLA_ROPE,)),
        "mla_out_norm_w": gain(ks[18], (MLA_WIDTH,)),
        "w_out": nrm(ks[19], (D_MIX, D_MODEL), D_MIX ** -0.5),
    }


def reference(x, c, positions, norm_w, w_ada, b_ada, w_in, ret_decay_logit_fwd,
              ret_decay_logit_bwd, ret_gn_w, q_norm_w, w_uq, kv_norm_w, w_ukv,
              qn_nope_w, qn_rope_w, kn_nope_w, kn_rope_w, mla_out_norm_w, w_out):
    f32 = jnp.float32
    B, S, _ = x.shape
    for _layer in range(DEPTH):
        mod = jax.nn.silu(c) @ w_ada + b_ada
        shift, scale, gate = jnp.split(mod, 3, axis=-1)
        h = rms_norm(x, norm_w) * (1.0 + scale[:, None, :]) + shift[:, None, :]

        proj = h @ w_in
        r_q, r_k, r_v, r_g, m_cq, m_ckv, m_kr, m_g = split_cols(proj, SPLIT_SIZES)

        rq = rope(r_q.reshape(B, S, RET_HEADS, RET_QK_DIM), positions)
        rk = rope(r_k.reshape(B, S, RET_HEADS, RET_QK_DIM), positions) * (RET_QK_DIM ** -0.5)
        rv = r_v.reshape(B, S, RET_HEADS, RET_V_DIM)
        rq = rq.transpose(0, 2, 1, 3).astype(f32)
        rk = rk.transpose(0, 2, 1, 3).astype(f32)
        rv = rv.transpose(0, 2, 1, 3).astype(f32)
        lg_f = jax.nn.log_sigmoid(ret_decay_logit_fwd.astype(f32))
        lg_b = jax.nn.log_sigmoid(ret_decay_logit_bwd.astype(f32))
        o_fwd = retention_dir(rq, rk, rv, lg_f, True)
        o_bwd = jnp.flip(retention_dir(jnp.flip(rq, 2), jnp.flip(rk, 2), jnp.flip(rv, 2),
                                       lg_b, False), 2)
        o = o_fwd + o_bwd
        mu = jnp.mean(o, axis=-1, keepdims=True)
        var = jnp.mean(jnp.square(o - mu), axis=-1, keepdims=True)
        o = (o - mu) * lax.rsqrt(var + EPS) * ret_gn_w.astype(f32)[None, :, None, :]
        o = o.transpose(0, 2, 1, 3).reshape(B, S, RET_WIDTH).astype(x.dtype)
        ret_out = o * jax.nn.silu(r_g)

        cq = rms_norm(m_cq, q_norm_w)
        q = (cq @ w_uq).reshape(B, S, MLA_HEADS, MLA_NOPE + MLA_ROPE)
        q_nope, q_rope = q[..., :MLA_NOPE], q[..., MLA_NOPE:]
        ckv = rms_norm(m_ckv, kv_norm_w)
        kv = (ckv @ w_ukv).reshape(B, S, MLA_HEADS, MLA_NOPE + MLA_V)
        k_nope, v = kv[..., :MLA_NOPE], kv[..., MLA_NOPE:]
        q_nope = rms_norm(q_nope, qn_nope_w)
        q_rope = rope(rms_norm(q_rope, qn_rope_w), positions)
        k_nope = rms_norm(k_nope, kn_nope_w)
        k_rope = rope(rms_norm(m_kr, kn_rope_w), positions)
        sm_scale = (MLA_NOPE + MLA_ROPE) ** -0.5
        nb = S // Q_BLOCK
        qn_blk = jnp.moveaxis(q_nope.reshape(B, nb, Q_BLOCK, MLA_HEADS, MLA_NOPE), 1, 0)
        qr_blk = jnp.moveaxis(q_rope.reshape(B, nb, Q_BLOCK, MLA_HEADS, MLA_ROPE), 1, 0)

        def attend(blk):
            qn, qr = blk
            s = (jnp.einsum('bqhd,bkhd->bhqk', qn, k_nope)
                 + jnp.einsum('bqhd,bkd->bhqk', qr, k_rope))
            p = jax.nn.softmax(s.astype(f32) * sm_scale, axis=-1).astype(v.dtype)
            return jnp.einsum('bhqk,bkhd->bqhd', p, v)

        att = lax.map(attend, (qn_blk, qr_blk))
        att = jnp.moveaxis(att, 0, 1).reshape(B, S, MLA_WIDTH)
        mla_out = rms_norm(att, mla_out_norm_w) * jax.nn.silu(m_g)

        y = jnp.concatenate([ret_out, mla_out], axis=-1) @ w_out
        x = x + gate[:, None, :] * y
    return x
```

```python
import functools
import math

import jax
import jax.numpy as jnp
from jax import lax
from jax.experimental import pallas as pl
from jax.experimental.pallas import tpu as pltpu

RET_HEADS = 8
RET_QK = 64
RET_V = 128
MLA_HEADS = 8
MLA_NOPE = 64
MLA_ROPE = 32
MLA_V = 128
Q_LORA = 384
KV_LORA = 256
ROPE_BASE = 10000.0
EPS = 1e-6
HEAD_PAD = 128
LOG2E = 1.4426950408889634

_SIZES = (RET_HEADS * RET_QK, RET_HEADS * RET_QK, RET_HEADS * RET_V, RET_HEADS * RET_V,
          Q_LORA, KV_LORA, MLA_ROPE, MLA_HEADS * MLA_V)
_OFFS = tuple(sum(_SIZES[:i]) for i in range(len(_SIZES) + 1))

VMEM_LIMIT = 56 * 1024 * 1024

_NT = (((1,), (1,)), ((), ()))
_TN = (((0,), (0,)), ((), ()))
_F32 = jnp.float32
_BF16 = jnp.bfloat16


def _silu(v):
    return v * (1.0 / (1.0 + jnp.exp(-v)))


def _params(sem):
    return pltpu.CompilerParams(dimension_semantics=sem, vmem_limit_bytes=VMEM_LIMIT)


def _ada_kernel(c_ref, w_ref, b_ref, o_ref):
    a = _silu(c_ref[...]).astype(_BF16)
    o_ref[...] = jnp.dot(a, w_ref[...].astype(_BF16), preferred_element_type=_F32) + b_ref[...]


def _ada(c, w_ada, b_ada):
    bsz, d = c.shape
    n = w_ada.shape[1]
    tn = 1024
    return pl.pallas_call(
        _ada_kernel,
        grid=(n // tn,),
        in_specs=[pl.BlockSpec((bsz, d), lambda j: (0, 0)),
                  pl.BlockSpec((d, tn), lambda j: (0, j)),
                  pl.BlockSpec((1, tn), lambda j: (0, j))],
        out_specs=pl.BlockSpec((bsz, tn), lambda j: (0, j)),
        out_shape=jax.ShapeDtypeStruct((bsz, n), _F32),
        compiler_params=_params(("arbitrary",)),
        name="ada_mod",
    )(c, w_ada, b_ada.reshape(1, n))


def _inproj_kernel(x_ref, pos_ref, mod_ref, nw_ref, w_ref, inv64_ref, inv32_ref,
                   qnw_ref, kvnw_ref, krw_ref,
                   rq_ref, rk_ref, rv_ref, rg_ref, cq_ref, ckv_ref, kr_ref, mg_ref,
                   cos_ref, sin_ref):
    x = x_ref[0]
    shift = mod_ref[0, 0:1, :]
    scale = mod_ref[0, 1:2, :]
    y = x * lax.rsqrt(jnp.mean(x * x, axis=-1, keepdims=True) + EPS) * nw_ref[...]
    h = (y * (1.0 + scale) + shift).astype(_BF16)

    def proj(i):
        return lax.dot_general(w_ref[_OFFS[i]:_OFFS[i + 1], :], h, _NT,
                               preferred_element_type=_F32)

    pos = pos_ref[0].astype(_F32)
    ang64 = inv64_ref[...] * pos
    cos64, sin64 = jnp.cos(ang64), jnp.sin(ang64)
    ang32 = inv32_ref[...] * pos
    cos32, sin32 = jnp.cos(ang32), jnp.sin(ang32)
    cos_ref[0] = cos32
    sin_ref[0] = sin32

    half = RET_QK // 2
    for src, dst, mul in ((0, rq_ref, 1.0), (1, rk_ref, RET_QK ** -0.5)):
        p = proj(src)
        for hd in range(RET_HEADS):
            x1 = p[hd * RET_QK: hd * RET_QK + half]
            x2 = p[hd * RET_QK + half: (hd + 1) * RET_QK]
            dst[0, hd * RET_QK: hd * RET_QK + half, :] = ((x1 * cos64 - x2 * sin64) * mul).astype(_BF16)
            dst[0, hd * RET_QK + half: (hd + 1) * RET_QK, :] = ((x2 * cos64 + x1 * sin64) * mul).astype(_BF16)

    rv_ref[0] = proj(2).astype(_BF16)
    rg_ref[0] = _silu(proj(3)).astype(_BF16)

    cq = proj(4)
    cq_ref[0] = (cq * lax.rsqrt(jnp.mean(cq * cq, axis=0, keepdims=True) + EPS) * qnw_ref[...]).astype(_BF16)
    ckv = proj(5)
    ckv_ref[0] = (ckv * lax.rsqrt(jnp.mean(ckv * ckv, axis=0, keepdims=True) + EPS) * kvnw_ref[...]).astype(_BF16)

    kr = proj(6)
    kr = kr * lax.rsqrt(jnp.mean(kr * kr, axis=0, keepdims=True) + EPS) * krw_ref[...]
    k1, k2 = kr[:MLA_ROPE // 2], kr[MLA_ROPE // 2:]
    kr_ref[0, :MLA_ROPE // 2, :] = (k1 * cos32 - k2 * sin32).astype(_BF16)
    kr_ref[0, MLA_ROPE // 2:, :] = (k2 * cos32 + k1 * sin32).astype(_BF16)

    mg_ref[0] = _silu(proj(7)).astype(_BF16)


def _inproj(x, pos3, mod3, norm_w, w_in_t, inv64, inv32, qnw, kvnw, krw, tm):
    bsz, s, d = x.shape
    grid = (bsz, s // tm)
    const = lambda b, i: (0, 0)
    tok = lambda b, i: (b, 0, i)
    sizes = (_SIZES[0], _SIZES[1], _SIZES[2], _SIZES[3], _SIZES[4], _SIZES[5], _SIZES[6], _SIZES[7])
    out_shape = [jax.ShapeDtypeStruct((bsz, n, s), _BF16) for n in sizes]
    out_shape += [jax.ShapeDtypeStruct((bsz, MLA_ROPE // 2, s), _F32)] * 2
    out_specs = [pl.BlockSpec((1, n, tm), tok) for n in sizes]
    out_specs += [pl.BlockSpec((1, MLA_ROPE // 2, tm), tok)] * 2
    return pl.pallas_call(
        _inproj_kernel,
        grid=grid,
        in_specs=[pl.BlockSpec((1, tm, d), lambda b, i: (b, i, 0)),
                  pl.BlockSpec((1, 1, tm), tok),
                  pl.BlockSpec((1, 3, d), lambda b, i: (b, 0, 0)),
                  pl.BlockSpec((1, d), const),
                  pl.BlockSpec(w_in_t.shape, const),
                  pl.BlockSpec(inv64.shape, const),
                  pl.BlockSpec(inv32.shape, const),
                  pl.BlockSpec(qnw.shape, const),
                  pl.BlockSpec(kvnw.shape, const),
                  pl.BlockSpec(krw.shape, const)],
        out_specs=out_specs,
        out_shape=out_shape,
        compiler_params=_params(("parallel", "parallel")),
        name="in_proj",
    )(x, pos3, mod3, norm_w.reshape(1, d), w_in_t, inv64, inv32, qnw, kvnw, krw)


def _log_sigmoid(v):
    return jnp.minimum(v, 0.0) - jnp.log(1.0 + jnp.exp(-jnp.abs(v)))


def _retention_kernel(lf_ref, lb_ref, q_ref, k_ref, v_ref, g_ref, gnw_ref, o_ref,
                      kv_ref, st_ref, *, chunk, n_chunks):
    hd = pl.program_id(1)
    c = chunk
    lgf = _log_sigmoid(jnp.full((1, 1), lf_ref[hd], _F32))
    lgb = _log_sigmoid(jnp.full((1, 1), lb_ref[hd], _F32))

    pos_l = lax.broadcasted_iota(jnp.int32, (1, c), 1).astype(_F32)
    kdec_f = jnp.exp(lgf * (c - 1.0 - pos_l))
    kdec_b = jnp.exp(lgb * pos_l)
    qdec_f = jnp.exp(lgf * (pos_l + 1.0))
    qdec_b = jnp.exp(lgb * (c - pos_l))
    jj = lax.broadcasted_iota(jnp.int32, (c, c), 0).astype(_F32)
    ii = lax.broadcasted_iota(jnp.int32, (c, c), 1).astype(_F32)
    dmat = jnp.where(ii >= jj, jnp.exp(lgf * jnp.maximum(ii - jj, 0.0)),
                     jnp.exp(lgb * jnp.maximum(jj - ii, 0.0)))
    lane = lax.broadcasted_iota(jnp.int32, (1, 2 * RET_QK), 1)
    is_f = lane < RET_QK
    sdec = jnp.where(is_f, jnp.exp(lgf * c), jnp.exp(lgb * c))

    def chunk_ds(i):
        return pl.ds(pl.multiple_of(i * c, c), c)

    def kv_body(i, carry):
        ds = chunk_ds(i)
        kc = k_ref[0, :, ds].astype(_F32)
        kfb = jnp.concatenate([kc * kdec_f, kc * kdec_b], axis=0).astype(_BF16)
        kv_ref[i] = lax.dot_general(v_ref[0, :, ds], kfb, _NT, preferred_element_type=_F32)
        return carry

    lax.fori_loop(0, n_chunks, kv_body, 0)

    def bwd_body(t, sb):
        i = n_chunks - 1 - t
        st_ref[i] = sb
        return sb * sdec + kv_ref[i]

    lax.fori_loop(0, n_chunks, bwd_body, jnp.zeros((RET_V, 2 * RET_QK), _F32))

    gnw = gnw_ref[0]

    def out_body(i, sf):
        ds = chunk_ds(i)
        qc = q_ref[0, :, ds]
        kc = k_ref[0, :, ds]
        st = lax.dot_general(kc, qc, _TN, preferred_element_type=_F32)
        pt = (st * dmat).astype(_BF16)
        qf = qc.astype(_F32)
        qfb = jnp.concatenate([qf * qdec_f, qf * qdec_b], axis=0).astype(_BF16)
        state = jnp.where(is_f, sf, st_ref[i]).astype(_BF16)
        o = (jnp.dot(v_ref[0, :, ds], pt, preferred_element_type=_F32)
             + jnp.dot(state, qfb, preferred_element_type=_F32))
        mu = jnp.mean(o, axis=0, keepdims=True)
        dlt = o - mu
        var = jnp.mean(dlt * dlt, axis=0, keepdims=True)
        on = dlt * lax.rsqrt(var + EPS) * gnw
        o_ref[0, :, ds] = (on * g_ref[0, :, ds].astype(_F32)).astype(_BF16)
        return sf * sdec + kv_ref[i]

    lax.fori_loop(0, n_chunks, out_body, jnp.zeros((RET_V, 2 * RET_QK), _F32))


def _retention(lf, lb, rq_t, rk_t, rv_t, rg_t, gnw3, chunk):
    bsz, _, s = rq_t.shape
    n_chunks = s // chunk
    smem = pl.BlockSpec(memory_space=pltpu.SMEM)
    return pl.pallas_call(
        functools.partial(_retention_kernel, chunk=chunk, n_chunks=n_chunks),
        grid=(bsz, RET_HEADS),
        in_specs=[smem, smem,
                  pl.BlockSpec((1, RET_QK, s), lambda b, h: (b, h, 0)),
                  pl.BlockSpec((1, RET_QK, s), lambda b, h: (b, h, 0)),
                  pl.BlockSpec((1, RET_V, s), lambda b, h: (b, h, 0)),
                  pl.BlockSpec((1, RET_V, s), lambda b, h: (b, h, 0)),
                  pl.BlockSpec((1, RET_V, 1), lambda b, h: (h, 0, 0))],
        out_specs=pl.BlockSpec((1, RET_V, s), lambda b, h: (b, h, 0)),
        out_shape=jax.ShapeDtypeStruct((bsz, RET_HEADS * RET_V, s), _BF16),
        scratch_shapes=[pltpu.VMEM((n_chunks, RET_V, 2 * RET_QK), _F32),
                        pltpu.VMEM((n_chunks, RET_V, 2 * RET_QK), _F32)],
        compiler_params=_params(("parallel", "parallel")),
        name="retention",
    )(lf, lb, rq_t, rk_t, rv_t, rg_t, gnw3)


def _mla_up_kernel(cq_ref, ckv_ref, kr_ref, cos_ref, sin_ref, wq_ref, wkv_ref,
                   qnw_ref, qrw_ref, knw_ref, qt_ref, k_ref, vt_ref, *, q_mul):
    tm = cq_ref.shape[2]
    q_all = jnp.dot(wq_ref[...], cq_ref[0], preferred_element_type=_F32)
    kv_all = jnp.dot(wkv_ref[...], ckv_ref[0], preferred_element_type=_F32)
    cos, sin = cos_ref[0], sin_ref[0]
    kr = kr_ref[0].astype(_F32)
    zpad = jnp.zeros((HEAD_PAD - MLA_NOPE - MLA_ROPE, tm), _F32)
    hq = MLA_NOPE + MLA_ROPE
    hkv = MLA_NOPE + MLA_V
    r2 = MLA_ROPE // 2
    for hd in range(MLA_HEADS):
        qn = q_all[hd * hq: hd * hq + MLA_NOPE]
        qr = q_all[hd * hq + MLA_NOPE: (hd + 1) * hq]
        qn = qn * lax.rsqrt(jnp.mean(qn * qn, axis=0, keepdims=True) + EPS) * qnw_ref[...]
        qr = qr * lax.rsqrt(jnp.mean(qr * qr, axis=0, keepdims=True) + EPS) * qrw_ref[...]
        q1, q2 = qr[:r2], qr[r2:]
        qh = jnp.concatenate([qn, q1 * cos - q2 * sin, q2 * cos + q1 * sin, zpad], axis=0)
        qt_ref[0, hd] = (qh * q_mul).astype(_BF16)
        kn = kv_all[hd * hkv: hd * hkv + MLA_NOPE]
        kn = kn * lax.rsqrt(jnp.mean(kn * kn, axis=0, keepdims=True) + EPS) * knw_ref[...]
        kh = jnp.concatenate([kn, kr, zpad], axis=0)
        k_ref[0, hd] = kh.T.astype(_BF16)
        vt_ref[0, hd] = kv_all[hd * hkv + MLA_NOPE: (hd + 1) * hkv].astype(_BF16)


def _mla_up(cq_t, ckv_t, kr_t, cos_t, sin_t, wq_t, wkv_t, qnw, qrw, knw, tm):
    bsz, _, s = cq_t.shape
    tok = lambda b, i: (b, 0, i)
    const = lambda b, i: (0, 0)
    q_mul = (MLA_NOPE + MLA_ROPE) ** -0.5 * LOG2E
    return pl.pallas_call(
        functools.partial(_mla_up_kernel, q_mul=q_mul),
        grid=(bsz, s // tm),
        in_specs=[pl.BlockSpec((1, Q_LORA, tm), tok),
                  pl.BlockSpec((1, KV_LORA, tm), tok),
                  pl.BlockSpec((1, MLA_ROPE, tm), tok),
                  pl.BlockSpec((1, MLA_ROPE // 2, tm), tok),
                  pl.BlockSpec((1, MLA_ROPE // 2, tm), tok),
                  pl.BlockSpec(wq_t.shape, const),
                  pl.BlockSpec(wkv_t.shape, const),
                  pl.BlockSpec(qnw.shape, const),
                  pl.BlockSpec(qrw.shape, const),
                  pl.BlockSpec(knw.shape, const)],
        out_specs=[pl.BlockSpec((1, MLA_HEADS, HEAD_PAD, tm), lambda b, i: (b, 0, 0, i)),
                   pl.BlockSpec((1, MLA_HEADS, tm, HEAD_PAD), lambda b, i: (b, 0, i, 0)),
                   pl.BlockSpec((1, MLA_HEADS, MLA_V, tm), lambda b, i: (b, 0, 0, i))],
        out_shape=[jax.ShapeDtypeStruct((bsz, MLA_HEADS, HEAD_PAD, s), _BF16),
                   jax.ShapeDtypeStruct((bsz, MLA_HEADS, s, HEAD_PAD), _BF16),
                   jax.ShapeDtypeStruct((bsz, MLA_HEADS, MLA_V, s), _BF16)],
        compiler_params=_params(("parallel", "parallel")),
        name="mla_up",
    )(cq_t, ckv_t, kr_t, cos_t, sin_t, wq_t, wkv_t, qnw, qrw, knw)


def _attn_kernel(qt_ref, k_ref, vt_ref, o_ref, *, tk, n_kv):
    qt = qt_ref[0, 0]
    tq = qt.shape[1]

    def body(j, carry):
        m, l, acc = carry
        ds = pl.ds(pl.multiple_of(j * tk, tk), tk)
        st = jnp.dot(k_ref[0, 0, ds, :], qt, preferred_element_type=_F32)
        m_new = jnp.maximum(m, jnp.max(st, axis=0, keepdims=True))
        alpha = jnp.exp2(m - m_new)
        p = jnp.exp2(st - m_new)
        l = alpha * l + jnp.sum(p, axis=0, keepdims=True)
        acc = alpha * acc + jnp.dot(vt_ref[0, 0, :, ds], p.astype(_BF16), preferred_element_type=_F32)
        return m_new, l, acc

    m0 = jnp.full((1, tq), -jnp.inf, _F32)
    l0 = jnp.zeros((1, tq), _F32)
    a0 = jnp.zeros((MLA_V, tq), _F32)
    _, l, acc = lax.fori_loop(0, n_kv, body, (m0, l0, a0))
    o_ref[0] = (acc * (1.0 / l)).astype(_BF16)


def _attention(qt, k, vt, tq, tk):
    bsz, nh, _, s = qt.shape
    return pl.pallas_call(
        functools.partial(_attn_kernel, tk=tk, n_kv=s // tk),
        grid=(bsz, nh, s // tq),
        in_specs=[pl.BlockSpec((1, 1, HEAD_PAD, tq), lambda b, h, i: (b, h, 0, i)),
                  pl.BlockSpec((1, 1, s, HEAD_PAD), lambda b, h, i: (b, h, 0, 0)),
                  pl.BlockSpec((1, 1, MLA_V, s), lambda b, h, i: (b, h, 0, 0))],
        out_specs=pl.BlockSpec((1, MLA_V, tq), lambda b, h, i: (b, h, i)),
        out_shape=jax.ShapeDtypeStruct((bsz, nh * MLA_V, s), _BF16),
        compiler_params=_params(("parallel", "parallel", "parallel")),
        name="mla_attention",
    )(qt, k, vt)


def _out_kernel(x_ref, ret_ref, att_ref, mg_ref, mod_ref, onw_ref, w_ref, o_ref):
    att = att_ref[0].astype(_F32)
    mla = att * lax.rsqrt(jnp.mean(att * att, axis=0, keepdims=True) + EPS) * onw_ref[...]
    mla = (mla * mg_ref[0].astype(_F32)).astype(_BF16)
    cat = jnp.concatenate([ret_ref[0], mla], axis=0)
    yt = jnp.dot(w_ref[...], cat, preferred_element_type=_F32)
    gate = mod_ref[0, 2:3, :]
    o_ref[0] = x_ref[0] + gate * yt.T


def _out_proj(x, ret_t, att_t, mg_t, mod3, onw, w_out_t, tm):
    bsz, s, d = x.shape
    tok = lambda b, i: (b, 0, i)
    const = lambda b, i: (0, 0)
    return pl.pallas_call(
        _out_kernel,
        grid=(bsz, s // tm),
        in_specs=[pl.BlockSpec((1, tm, d), lambda b, i: (b, i, 0)),
                  pl.BlockSpec((1, ret_t.shape[1], tm), tok),
                  pl.BlockSpec((1, att_t.shape[1], tm), tok),
                  pl.BlockSpec((1, mg_t.shape[1], tm), tok),
                  pl.BlockSpec((1, 3, d), lambda b, i: (b, 0, 0)),
                  pl.BlockSpec(onw.shape, const),
                  pl.BlockSpec(w_out_t.shape, const)],
        out_specs=pl.BlockSpec((1, tm, d), lambda b, i: (b, i, 0)),
        out_shape=jax.ShapeDtypeStruct((bsz, s, d), x.dtype),
        compiler_params=_params(("parallel", "parallel")),
        name="out_proj",
    )(x, ret_t, att_t, mg_t, mod3, onw, w_out_t)


def _col(v):
    return v.astype(_F32).reshape(-1, 1)


def _block_sizes(s):
    tm = min(512, s)
    tq = min(512, s)
    tk = min(512, s)
    chunk = min(256, s)
    return tm, tq, tk, chunk


def kernel(x, c, positions, norm_w, w_ada, b_ada, w_in, ret_decay_logit_fwd, ret_decay_logit_bwd, ret_gn_w, q_norm_w, w_uq, kv_norm_w, w_ukv, qn_nope_w, qn_rope_w, kn_nope_w, kn_rope_w, mla_out_norm_w, w_out):
    bsz, s, d = x.shape
    tm, tq, tk, chunk = _block_sizes(s)

    w_in_t = w_in.T.astype(_BF16)
    wq_t = w_uq.T.astype(_BF16)
    wkv_t = w_ukv.T.astype(_BF16)
    w_out_t = w_out.T.astype(_BF16)
    inv64 = (ROPE_BASE ** (-jnp.arange(0, RET_QK, 2, dtype=_F32) / RET_QK)).reshape(-1, 1)
    inv32 = (ROPE_BASE ** (-jnp.arange(0, MLA_ROPE, 2, dtype=_F32) / MLA_ROPE)).reshape(-1, 1)

    mod3 = _ada(c, w_ada, b_ada).reshape(bsz, 3, d)
    pos3 = positions.reshape(bsz, 1, s)

    (rq_t, rk_t, rv_t, rg_t, cq_t, ckv_t, kr_t, mg_t, cos_t, sin_t) = _inproj(
        x, pos3, mod3, norm_w, w_in_t, inv64, inv32,
        _col(q_norm_w), _col(kv_norm_w), _col(kn_rope_w), tm)

    ret_t = _retention(ret_decay_logit_fwd.astype(_F32), ret_decay_logit_bwd.astype(_F32),
                       rq_t, rk_t, rv_t, rg_t,
                       ret_gn_w.astype(_F32).reshape(RET_HEADS, RET_V, 1), chunk)

    qt, k, vt = _mla_up(cq_t, ckv_t, kr_t, cos_t, sin_t, wq_t, wkv_t,
                        _col(qn_nope_w), _col(qn_rope_w), _col(kn_nope_w), tm)
    att_t = _attention(qt, k, vt, tq, tk)

    return _out_proj(x, ret_t, att_t, mg_t, mod3, _col(mla_out_norm_w), w_out_t, tm)
```

```python
import functools
import math

import jax
import jax.numpy as jnp
from jax import lax
from jax.experimental import pallas as pl
from jax.experimental.pallas import tpu as pltpu

RET_HEADS = 8
RET_QK = 64
RET_V = 128
MLA_HEADS = 8
MLA_NOPE = 64
MLA_ROPE = 32
MLA_V = 128
Q_LORA = 384
KV_LORA = 256
ROPE_BASE = 10000.0
EPS = 1e-6
HEAD_PAD = 128
LOG2E = 1.4426950408889634

_SIZES = (RET_HEADS * RET_QK, RET_HEADS * RET_QK, RET_HEADS * RET_V, RET_HEADS * RET_V,
          Q_LORA, KV_LORA, MLA_ROPE, MLA_HEADS * MLA_V)
_OFFS = tuple(sum(_SIZES[:i]) for i in range(len(_SIZES) + 1))

VMEM_LIMIT = 56 * 1024 * 1024

_NT = (((1,), (1,)), ((), ()))
_TN = (((0,), (0,)), ((), ()))
_F32 = jnp.float32
_BF16 = jnp.bfloat16


def _silu(v):
    return v * (1.0 / (1.0 + jnp.exp(-v)))


def _params(sem):
    return pltpu.CompilerParams(dimension_semantics=sem, vmem_limit_bytes=VMEM_LIMIT)


def _ada_kernel(c_ref, w_ref, b_ref, o_ref):
    a = _silu(c_ref[...]).astype(_BF16)
    o_ref[...] = jnp.dot(a, w_ref[...].astype(_BF16), preferred_element_type=_F32) + b_ref[...]


def _ada(c, w_ada, b_ada):
    bsz, d = c.shape
    n = w_ada.shape[1]
    tn = 1024
    return pl.pallas_call(
        _ada_kernel,
        grid=(n // tn,),
        in_specs=[pl.BlockSpec((bsz, d), lambda j: (0, 0)),
                  pl.BlockSpec((d, tn), lambda j: (0, j)),
                  pl.BlockSpec((1, tn), lambda j: (0, j))],
        out_specs=pl.BlockSpec((bsz, tn), lambda j: (0, j)),
        out_shape=jax.ShapeDtypeStruct((bsz, n), _F32),
        compiler_params=_params(("arbitrary",)),
        name="ada_mod",
    )(c, w_ada, b_ada.reshape(1, n))


def _inproj_kernel(x_ref, pos_ref, mod_ref, nw_ref, w_ref, inv64_ref, inv32_ref,
                   qnw_ref, kvnw_ref, krw_ref,
                   rq_ref, rk_ref, rv_ref, rg_ref, cq_ref, ckv_ref, kr_ref, mg_ref,
                   cos_ref, sin_ref):
    x = x_ref[0]
    shift = mod_ref[0, 0:1, :]
    scale = mod_ref[0, 1:2, :]
    y = x * lax.rsqrt(jnp.mean(x * x, axis=-1, keepdims=True) + EPS) * nw_ref[...]
    h = (y * (1.0 + scale) + shift).astype(_BF16)

    def proj(i):
        return lax.dot_general(w_ref[_OFFS[i]:_OFFS[i + 1], :], h, _NT,
                               preferred_element_type=_F32)

    pos = pos_ref[0].astype(_F32)
    ang64 = inv64_ref[...] * pos
    cos64, sin64 = jnp.cos(ang64), jnp.sin(ang64)
    ang32 = inv32_ref[...] * pos
    cos32, sin32 = jnp.cos(ang32), jnp.sin(ang32)
    cos_ref[0] = cos32
    sin_ref[0] = sin32

    half = RET_QK // 2
    for src, dst, mul in ((0, rq_ref, 1.0), (1, rk_ref, RET_QK ** -0.5)):
        p = proj(src)
        for hd in range(RET_HEADS):
            x1 = p[hd * RET_QK: hd * RET_QK + half]
            x2 = p[hd * RET_QK + half: (hd + 1) * RET_QK]
            dst[0, hd * RET_QK: hd * RET_QK + half, :] = ((x1 * cos64 - x2 * sin64) * mul).astype(_BF16)
            dst[0, hd * RET_QK + half: (hd + 1) * RET_QK, :] = ((x2 * cos64 + x1 * sin64) * mul).astype(_BF16)

    rv_ref[0] = proj(2).astype(_BF16)
    rg_ref[0] = _silu(proj(3)).astype(_BF16)

    cq = proj(4)
    cq_ref[0] = (cq * lax.rsqrt(jnp.mean(cq * cq, axis=0, keepdims=True) + EPS) * qnw_ref[...]).astype(_BF16)
    ckv = proj(5)
    ckv_ref[0] = (ckv * lax.rsqrt(jnp.mean(ckv * ckv, axis=0, keepdims=True) + EPS) * kvnw_ref[...]).astype(_BF16)

    kr = proj(6)
    kr = kr * lax.rsqrt(jnp.mean(kr * kr, axis=0, keepdims=True) + EPS) * krw_ref[...]
    k1, k2 = kr[:MLA_ROPE // 2], kr[MLA_ROPE // 2:]
    kr_ref[0, :MLA_ROPE // 2, :] = (k1 * cos32 - k2 * sin32).astype(_BF16)
    kr_ref[0, MLA_ROPE // 2:, :] = (k2 * cos32 + k1 * sin32).astype(_BF16)

    mg_ref[0] = _silu(proj(7)).astype(_BF16)


def _inproj(x, pos3, mod3, norm_w, w_in_t, inv64, inv32, qnw, kvnw, krw, tm):
    bsz, s, d = x.shape
    grid = (bsz, s // tm)
    const = lambda b, i: (0, 0)
    tok = lambda b, i: (b, 0, i)
    sizes = (_SIZES[0], _SIZES[1], _SIZES[2], _SIZES[3], _SIZES[4], _SIZES[5], _SIZES[6], _SIZES[7])
    out_shape = [jax.ShapeDtypeStruct((bsz, n, s), _BF16) for n in sizes]
    out_shape += [jax.ShapeDtypeStruct((bsz, MLA_ROPE // 2, s), _F32)] * 2
    out_specs = [pl.BlockSpec((1, n, tm), tok) for n in sizes]
    out_specs += [pl.BlockSpec((1, MLA_ROPE // 2, tm), tok)] * 2
    return pl.pallas_call(
        _inproj_kernel,
        grid=grid,
        in_specs=[pl.BlockSpec((1, tm, d), lambda b, i: (b, i, 0)),
                  pl.BlockSpec((1, 1, tm), tok),
                  pl.BlockSpec((1, 3, d), lambda b, i: (b, 0, 0)),
                  pl.BlockSpec((1, d), const),
                  pl.BlockSpec(w_in_t.shape, const),
                  pl.BlockSpec(inv64.shape, const),
                  pl.BlockSpec(inv32.shape, const),
                  pl.BlockSpec(qnw.shape, const),
                  pl.BlockSpec(kvnw.shape, const),
                  pl.BlockSpec(krw.shape, const)],
        out_specs=out_specs,
        out_shape=out_shape,
        compiler_params=_params(("parallel", "parallel")),
        name="in_proj",
    )(x, pos3, mod3, norm_w.reshape(1, d), w_in_t, inv64, inv32, qnw, kvnw, krw)


def _log_sigmoid(v):
    return jnp.minimum(v, 0.0) - jnp.log(1.0 + jnp.exp(-jnp.abs(v)))


def _retention_kernel(lf_ref, lb_ref, q_ref, k_ref, v_ref, g_ref, gnw_ref, o_ref,
                      kv_ref, st_ref, *, chunk, n_chunks):
    hd = pl.program_id(1)
    c = chunk
    lgf = _log_sigmoid(jnp.full((1, 1), lf_ref[hd], _F32))
    lgb = _log_sigmoid(jnp.full((1, 1), lb_ref[hd], _F32))

    pos_l = lax.broadcasted_iota(jnp.int32, (1, c), 1).astype(_F32)
    kdec_f = jnp.exp(lgf * (c - 1.0 - pos_l))
    kdec_b = jnp.exp(lgb * pos_l)
    qdec_f = jnp.exp(lgf * (pos_l + 1.0))
    qdec_b = jnp.exp(lgb * (c - pos_l))
    jj = lax.broadcasted_iota(jnp.int32, (c, c), 0).astype(_F32)
    ii = lax.broadcasted_iota(jnp.int32, (c, c), 1).astype(_F32)
    dmat = jnp.where(ii >= jj, jnp.exp(lgf * jnp.maximum(ii - jj, 0.0)),
                     jnp.exp(lgb * jnp.maximum(jj - ii, 0.0)))
    lane = lax.broadcasted_iota(jnp.int32, (1, 2 * RET_QK), 1)
    is_f = lane < RET_QK
    sdec = jnp.where(is_f, jnp.exp(lgf * c), jnp.exp(lgb * c))

    def chunk_ds(i):
        return pl.ds(pl.multiple_of(i * c, c), c)

    def kv_body(i, carry):
        ds = chunk_ds(i)
        kc = k_ref[0, :, ds].astype(_F32)
        kfb = jnp.concatenate([kc * kdec_f, kc * kdec_b], axis=0).astype(_BF16)
        kv_ref[i] = lax.dot_general(v_ref[0, :, ds], kfb, _NT, preferred_element_type=_F32)
        return carry

    lax.fori_loop(0, n_chunks, kv_body, 0)

    def bwd_body(t, sb):
        i = n_chunks - 1 - t
        st_ref[i] = sb
        return sb * sdec + kv_ref[i]

    lax.fori_loop(0, n_chunks, bwd_body, jnp.zeros((RET_V, 2 * RET_QK), _F32))

    gnw = gnw_ref[0]

    def out_body(i, sf):
        ds = chunk_ds(i)
        qc = q_ref[0, :, ds]
        kc = k_ref[0, :, ds]
        st = lax.dot_general(kc, qc, _TN, preferred_element_type=_F32)
        pt = (st * dmat).astype(_BF16)
        qf = qc.astype(_F32)
        qfb = jnp.concatenate([qf * qdec_f, qf * qdec_b], axis=0).astype(_BF16)
        state = jnp.where(is_f, sf, st_ref[i]).astype(_BF16)
        o = (jnp.dot(v_ref[0, :, ds], pt, preferred_element_type=_F32)
             + jnp.dot(state, qfb, preferred_element_type=_F32))
        mu = jnp.mean(o, axis=0, keepdims=True)
        dlt = o - mu
        var = jnp.mean(dlt * dlt, axis=0, keepdims=True)
        on = dlt * lax.rsqrt(var + EPS) * gnw
        o_ref[0, :, ds] = (on * g_ref[0, :, ds].astype(_F32)).astype(_BF16)
        return sf * sdec + kv_ref[i]

    lax.fori_loop(0, n_chunks, out_body, jnp.zeros((RET_V, 2 * RET_QK), _F32))


def _retention(lf, lb, rq_t, rk_t, rv_t, rg_t, gnw3, chunk):
    bsz, _, s = rq_t.shape
    n_chunks = s // chunk
    smem = pl.BlockSpec(memory_space=pltpu.SMEM)
    return pl.pallas_call(
        functools.partial(_retention_kernel, chunk=chunk, n_chunks=n_chunks),
        grid=(bsz, RET_HEADS),
        in_specs=[smem, smem,
                  pl.BlockSpec((1, RET_QK, s), lambda b, h: (b, h, 0)),
                  pl.BlockSpec((1, RET_QK, s), lambda b, h: (b, h, 0)),
                  pl.BlockSpec((1, RET_V, s), lambda b, h: (b, h, 0)),
                  pl.BlockSpec((1, RET_V, s), lambda b, h: (b, h, 0)),
                  pl.BlockSpec((1, RET_V, 1), lambda b, h: (h, 0, 0))],
        out_specs=pl.BlockSpec((1, RET_V, s), lambda b, h: (b, h, 0)),
        out_shape=jax.ShapeDtypeStruct((bsz, RET_HEADS * RET_V, s), _BF16),
        scratch_shapes=[pltpu.VMEM((n_chunks, RET_V, 2 * RET_QK), _F32),
                        pltpu.VMEM((n_chunks, RET_V, 2 * RET_QK), _F32)],
        compiler_params=_params(("parallel", "parallel")),
        name="retention",
    )(lf, lb, rq_t, rk_t, rv_t, rg_t, gnw3)


def _mla_up_kernel(cq_ref, ckv_ref, kr_ref, cos_ref, sin_ref, wq_ref, wkv_ref,
                   qnw_ref, qrw_ref, knw_ref, qt_ref, k_ref, vt_ref, *, q_mul):
    tm = cq_ref.shape[2]
    q_all = jnp.dot(wq_ref[...], cq_ref[0], preferred_element_type=_F32)
    kv_all = jnp.dot(wkv_ref[...], ckv_ref[0], preferred_element_type=_F32)
    cos, sin = cos_ref[0], sin_ref[0]
    kr = kr_ref[0].astype(_F32)
    zpad = jnp.zeros((HEAD_PAD - MLA_NOPE - MLA_ROPE, tm), _F32)
    hq = MLA_NOPE + MLA_ROPE
    hkv = MLA_NOPE + MLA_V
    r2 = MLA_ROPE // 2
    for hd in range(MLA_HEADS):
        qn = q_all[hd * hq: hd * hq + MLA_NOPE]
        qr = q_all[hd * hq + MLA_NOPE: (hd + 1) * hq]
        qn = qn * lax.rsqrt(jnp.mean(qn * qn, axis=0, keepdims=True) + EPS) * qnw_ref[...]
        qr = qr * lax.rsqrt(jnp.mean(qr * qr, axis=0, keepdims=True) + EPS) * qrw_ref[...]
        q1, q2 = qr[:r2], qr[r2:]
        qh = jnp.concatenate([qn, q1 * cos - q2 * sin, q2 * cos + q1 * sin, zpad], axis=0)
        qt_ref[0, hd] = (qh * q_mul).astype(_BF16)
        kn = kv_all[hd * hkv: hd * hkv + MLA_NOPE]
        kn = kn * lax.rsqrt(jnp.mean(kn * kn, axis=0, keepdims=True) + EPS) * knw_ref[...]
        kh = jnp.concatenate([kn, kr, zpad], axis=0)
        k_ref[0, hd] = kh.T.astype(_BF16)
        vt_ref[0, hd] = kv_all[hd * hkv + MLA_NOPE: (hd + 1) * hkv].astype(_BF16)


def _mla_up(cq_t, ckv_t, kr_t, cos_t, sin_t, wq_t, wkv_t, qnw, qrw, knw, tm):
    bsz, _, s = cq_t.shape
    tok = lambda b, i: (b, 0, i)
    const = lambda b, i: (0, 0)
    q_mul = (MLA_NOPE + MLA_ROPE) ** -0.5 * LOG2E
    return pl.pallas_call(
        functools.partial(_mla_up_kernel, q_mul=q_mul),
        grid=(bsz, s // tm),
        in_specs=[pl.BlockSpec((1, Q_LORA, tm), tok),
                  pl.BlockSpec((1, KV_LORA, tm), tok),
                  pl.BlockSpec((1, MLA_ROPE, tm), tok),
                  pl.BlockSpec((1, MLA_ROPE // 2, tm), tok),
                  pl.BlockSpec((1, MLA_ROPE // 2, tm), tok),
                  pl.BlockSpec(wq_t.shape, const),
                  pl.BlockSpec(wkv_t.shape, const),
                  pl.BlockSpec(qnw.shape, const),
                  pl.BlockSpec(qrw.shape, const),
                  pl.BlockSpec(knw.shape, const)],
        out_specs=[pl.BlockSpec((1, MLA_HEADS, HEAD_PAD, tm), lambda b, i: (b, 0, 0, i)),
                   pl.BlockSpec((1, MLA_HEADS, tm, HEAD_PAD), lambda b, i: (b, 0, i, 0)),
                   pl.BlockSpec((1, MLA_HEADS, MLA_V, tm), lambda b, i: (b, 0, 0, i))],
        out_shape=[jax.ShapeDtypeStruct((bsz, MLA_HEADS, HEAD_PAD, s), _BF16),
                   jax.ShapeDtypeStruct((bsz, MLA_HEADS, s, HEAD_PAD), _BF16),
                   jax.ShapeDtypeStruct((bsz, MLA_HEADS, MLA_V, s), _BF16)],
        compiler_params=_params(("parallel", "parallel")),
        name="mla_up",
    )(cq_t, ckv_t, kr_t, cos_t, sin_t, wq_t, wkv_t, qnw, qrw, knw)


def _attn_kernel(qt_ref, k_ref, vt_ref, o_ref, sa_ref, sb_ref, *, tk, n_kv):
    qt = qt_ref[0, 0]
    tq = qt.shape[1]

    def kv_ds(j):
        return pl.ds(pl.multiple_of(j * tk, tk), tk)

    def scores(j, m, s_ref):
        st = jnp.dot(k_ref[0, 0, kv_ds(j), :], qt, preferred_element_type=_F32)
        s_ref[...] = st
        return jnp.maximum(m, jnp.max(st, axis=0, keepdims=True))

    def accumulate(j, m_prev, m, l, acc, s_ref):
        alpha = jnp.exp2(m_prev - m)
        p = jnp.exp2(s_ref[...] - m)
        l = alpha * l + jnp.sum(p, axis=0, keepdims=True)
        acc = alpha * acc + jnp.dot(vt_ref[0, 0, :, kv_ds(j)], p.astype(_BF16),
                                    preferred_element_type=_F32)
        return l, acc

    def body(i, carry):
        m_prev, m, l, acc = carry
        j = 2 * i
        m1 = scores(j + 1, m, sb_ref)
        l, acc = accumulate(j, m_prev, m, l, acc, sa_ref)
        m2 = scores(j + 2, m1, sa_ref)
        l, acc = accumulate(j + 1, m, m1, l, acc, sb_ref)
        return m1, m2, l, acc

    m_init = jnp.full((1, tq), -jnp.inf, _F32)
    l0 = jnp.zeros((1, tq), _F32)
    a0 = jnp.zeros((MLA_V, tq), _F32)
    m0 = scores(0, m_init, sa_ref)
    m_prev, m, l, acc = lax.fori_loop(0, n_kv // 2 - 1, body, (m_init, m0, l0, a0))
    m1 = scores(n_kv - 1, m, sb_ref)
    l, acc = accumulate(n_kv - 2, m_prev, m, l, acc, sa_ref)
    l, acc = accumulate(n_kv - 1, m, m1, l, acc, sb_ref)
    o_ref[0] = (acc * (1.0 / l)).astype(_BF16)


def _attention(qt, k, vt, tq, tk):
    bsz, nh, _, s = qt.shape
    return pl.pallas_call(
        functools.partial(_attn_kernel, tk=tk, n_kv=s // tk),
        grid=(bsz, nh, s // tq),
        in_specs=[pl.BlockSpec((1, 1, HEAD_PAD, tq), lambda b, h, i: (b, h, 0, i)),
                  pl.BlockSpec((1, 1, s, HEAD_PAD), lambda b, h, i: (b, h, 0, 0)),
                  pl.BlockSpec((1, 1, MLA_V, s), lambda b, h, i: (b, h, 0, 0))],
        out_specs=pl.BlockSpec((1, MLA_V, tq), lambda b, h, i: (b, h, i)),
        out_shape=jax.ShapeDtypeStruct((bsz, nh * MLA_V, s), _BF16),
        scratch_shapes=[pltpu.VMEM((tk, tq), _F32), pltpu.VMEM((tk, tq), _F32)],
        compiler_params=_params(("parallel", "parallel", "parallel")),
        name="mla_attention",
    )(qt, k, vt)


def _out_kernel(x_ref, ret_ref, att_ref, mg_ref, mod_ref, onw_ref, w_ref, o_ref):
    att = att_ref[0].astype(_F32)
    mla = att * lax.rsqrt(jnp.mean(att * att, axis=0, keepdims=True) + EPS) * onw_ref[...]
    mla = (mla * mg_ref[0].astype(_F32)).astype(_BF16)
    cat = jnp.concatenate([ret_ref[0], mla], axis=0)
    yt = jnp.dot(w_ref[...], cat, preferred_element_type=_F32)
    gate = mod_ref[0, 2:3, :]
    o_ref[0] = x_ref[0] + gate * yt.T


def _out_proj(x, ret_t, att_t, mg_t, mod3, onw, w_out_t, tm):
    bsz, s, d = x.shape
    tok = lambda b, i: (b, 0, i)
    const = lambda b, i: (0, 0)
    return pl.pallas_call(
        _out_kernel,
        grid=(bsz, s // tm),
        in_specs=[pl.BlockSpec((1, tm, d), lambda b, i: (b, i, 0)),
                  pl.BlockSpec((1, ret_t.shape[1], tm), tok),
                  pl.BlockSpec((1, att_t.shape[1], tm), tok),
                  pl.BlockSpec((1, mg_t.shape[1], tm), tok),
                  pl.BlockSpec((1, 3, d), lambda b, i: (b, 0, 0)),
                  pl.BlockSpec(onw.shape, const),
                  pl.BlockSpec(w_out_t.shape, const)],
        out_specs=pl.BlockSpec((1, tm, d), lambda b, i: (b, i, 0)),
        out_shape=jax.ShapeDtypeStruct((bsz, s, d), x.dtype),
        compiler_params=_params(("parallel", "parallel")),
        name="out_proj",
    )(x, ret_t, att_t, mg_t, mod3, onw, w_out_t)


def _col(v):
    return v.astype(_F32).reshape(-1, 1)


def _block_sizes(s):
    tm = min(512, s)
    tq = min(512, s)
    tk = min(1024, s)
    chunk = min(256, s)
    return tm, tq, tk, chunk


def kernel(x, c, positions, norm_w, w_ada, b_ada, w_in, ret_decay_logit_fwd, ret_decay_logit_bwd, ret_gn_w, q_norm_w, w_uq, kv_norm_w, w_ukv, qn_nope_w, qn_rope_w, kn_nope_w, kn_rope_w, mla_out_norm_w, w_out):
    bsz, s, d = x.shape
    tm, tq, tk, chunk = _block_sizes(s)

    w_in_t = w_in.T.astype(_BF16)
    wq_t = w_uq.T.astype(_BF16)
    wkv_t = w_ukv.T.astype(_BF16)
    w_out_t = w_out.T.astype(_BF16)
    inv64 = (ROPE_BASE ** (-jnp.arange(0, RET_QK, 2, dtype=_F32) / RET_QK)).reshape(-1, 1)
    inv32 = (ROPE_BASE ** (-jnp.arange(0, MLA_ROPE, 2, dtype=_F32) / MLA_ROPE)).reshape(-1, 1)

    mod3 = _ada(c, w_ada, b_ada).reshape(bsz, 3, d)
    pos3 = positions.reshape(bsz, 1, s)

    (rq_t, rk_t, rv_t, rg_t, cq_t, ckv_t, kr_t, mg_t, cos_t, sin_t) = _inproj(
        x, pos3, mod3, norm_w, w_in_t, inv64, inv32,
        _col(q_norm_w), _col(kv_norm_w), _col(kn_rope_w), tm)

    ret_t = _retention(ret_decay_logit_fwd.astype(_F32), ret_decay_logit_bwd.astype(_F32),
                       rq_t, rk_t, rv_t, rg_t,
                       ret_gn_w.astype(_F32).reshape(RET_HEADS, RET_V, 1), chunk)

    qt, k, vt = _mla_up(cq_t, ckv_t, kr_t, cos_t, sin_t, wq_t, wkv_t,
                        _col(qn_nope_w), _col(qn_rope_w), _col(kn_nope_w), tm)
    att_t = _attention(qt, k, vt, tq, tk)

    return _out_proj(x, ret_t, att_t, mg_t, mod3, _col(mla_out_norm_w), w_out_t, tm)
```

```python
import functools
import math

import jax
import jax.numpy as jnp
from jax import lax
from jax.experimental import pallas as pl
from jax.experimental.pallas import tpu as pltpu

RET_HEADS = 8
RET_QK = 64
RET_V = 128
MLA_HEADS = 8
MLA_NOPE = 64
MLA_ROPE = 32
MLA_V = 128
Q_LORA = 384
KV_LORA = 256
ROPE_BASE = 10000.0
EPS = 1e-6
HEAD_PAD = 128
LOG2E = 1.4426950408889634

_SIZES = (RET_HEADS * RET_QK, RET_HEADS * RET_QK, RET_HEADS * RET_V, RET_HEADS * RET_V,
          Q_LORA, KV_LORA, MLA_ROPE, MLA_HEADS * MLA_V)
_OFFS = tuple(sum(_SIZES[:i]) for i in range(len(_SIZES) + 1))

VMEM_LIMIT = 56 * 1024 * 1024

_NT = (((1,), (1,)), ((), ()))
_TN = (((0,), (0,)), ((), ()))
_F32 = jnp.float32
_BF16 = jnp.bfloat16


def _silu(v):
    return v * (0.5 + 0.5 * jnp.tanh(0.5 * v))


def _params(sem):
    return pltpu.CompilerParams(dimension_semantics=sem, vmem_limit_bytes=VMEM_LIMIT)


def _ada_kernel(c_ref, w_ref, b_ref, o_ref):
    a = _silu(c_ref[...]).astype(_BF16)
    o_ref[...] = jnp.dot(a, w_ref[...].astype(_BF16), preferred_element_type=_F32) + b_ref[...]


def _ada(c, w_ada, b_ada):
    bsz, d = c.shape
    n = w_ada.shape[1]
    tn = 1024
    return pl.pallas_call(
        _ada_kernel,
        grid=(n // tn,),
        in_specs=[pl.BlockSpec((bsz, d), lambda j: (0, 0)),
                  pl.BlockSpec((d, tn), lambda j: (0, j)),
                  pl.BlockSpec((1, tn), lambda j: (0, j))],
        out_specs=pl.BlockSpec((bsz, tn), lambda j: (0, j)),
        out_shape=jax.ShapeDtypeStruct((bsz, n), _F32),
        compiler_params=_params(("arbitrary",)),
        name="ada_mod",
    )(c, w_ada, b_ada.reshape(1, n))


def _inproj_kernel(x_ref, pos_ref, mod_ref, nw_ref, w_ref, inv64_ref, inv32_ref,
                   qnw_ref, kvnw_ref, krw_ref,
                   rq_ref, rk_ref, rv_ref, rg_ref, cq_ref, ckv_ref, kr_ref, mg_ref,
                   cos_ref, sin_ref):
    x = x_ref[0]
    shift = mod_ref[0, 0:1, :]
    scale = mod_ref[0, 1:2, :]
    y = x * lax.rsqrt(jnp.mean(x * x, axis=-1, keepdims=True) + EPS) * nw_ref[...]
    h = (y * (1.0 + scale) + shift).astype(_BF16)

    def proj(i):
        return lax.dot_general(w_ref[_OFFS[i]:_OFFS[i + 1], :], h, _NT,
                               preferred_element_type=_F32)

    pos = pos_ref[0].astype(_F32)
    ang64 = inv64_ref[...] * pos
    cos64, sin64 = jnp.cos(ang64), jnp.sin(ang64)
    ang32 = inv32_ref[...] * pos
    cos32, sin32 = jnp.cos(ang32), jnp.sin(ang32)
    cos_ref[0] = cos32
    sin_ref[0] = sin32

    rg_ref[0] = _silu(proj(3)).astype(_BF16)
    mg_ref[0] = _silu(proj(7)).astype(_BF16)

    half = RET_QK // 2
    for src, dst, mul in ((0, rq_ref, 1.0), (1, rk_ref, RET_QK ** -0.5)):
        p = proj(src)
        for hd in range(RET_HEADS):
            x1 = p[hd * RET_QK: hd * RET_QK + half]
            x2 = p[hd * RET_QK + half: (hd + 1) * RET_QK]
            dst[0, hd * RET_QK: hd * RET_QK + half, :] = ((x1 * cos64 - x2 * sin64) * mul).astype(_BF16)
            dst[0, hd * RET_QK + half: (hd + 1) * RET_QK, :] = ((x2 * cos64 + x1 * sin64) * mul).astype(_BF16)

    cq = proj(4)
    cq_ref[0] = (cq * lax.rsqrt(jnp.mean(cq * cq, axis=0, keepdims=True) + EPS) * qnw_ref[...]).astype(_BF16)
    ckv = proj(5)
    ckv_ref[0] = (ckv * lax.rsqrt(jnp.mean(ckv * ckv, axis=0, keepdims=True) + EPS) * kvnw_ref[...]).astype(_BF16)

    kr = proj(6)
    kr = kr * lax.rsqrt(jnp.mean(kr * kr, axis=0, keepdims=True) + EPS) * krw_ref[...]
    k1, k2 = kr[:MLA_ROPE // 2], kr[MLA_ROPE // 2:]
    kr_ref[0, :MLA_ROPE // 2, :] = (k1 * cos32 - k2 * sin32).astype(_BF16)
    kr_ref[0, MLA_ROPE // 2:, :] = (k2 * cos32 + k1 * sin32).astype(_BF16)

    rv_ref[0] = proj(2).astype(_BF16)


def _inproj(x, pos3, mod3, norm_w, w_in_t, inv64, inv32, qnw, kvnw, krw, tm):
    bsz, s, d = x.shape
    grid = (bsz, s // tm)
    const = lambda b, i: (0, 0)
    tok = lambda b, i: (b, 0, i)
    sizes = (_SIZES[0], _SIZES[1], _SIZES[2], _SIZES[3], _SIZES[4], _SIZES[5], _SIZES[6], _SIZES[7])
    out_shape = [jax.ShapeDtypeStruct((bsz, n, s), _BF16) for n in sizes]
    out_shape += [jax.ShapeDtypeStruct((bsz, MLA_ROPE // 2, s), _F32)] * 2
    out_specs = [pl.BlockSpec((1, n, tm), tok) for n in sizes]
    out_specs += [pl.BlockSpec((1, MLA_ROPE // 2, tm), tok)] * 2
    return pl.pallas_call(
        _inproj_kernel,
        grid=grid,
        in_specs=[pl.BlockSpec((1, tm, d), lambda b, i: (b, i, 0)),
                  pl.BlockSpec((1, 1, tm), tok),
                  pl.BlockSpec((1, 3, d), lambda b, i: (b, 0, 0)),
                  pl.BlockSpec((1, d), const),
                  pl.BlockSpec(w_in_t.shape, const),
                  pl.BlockSpec(inv64.shape, const),
                  pl.BlockSpec(inv32.shape, const),
                  pl.BlockSpec(qnw.shape, const),
                  pl.BlockSpec(kvnw.shape, const),
                  pl.BlockSpec(krw.shape, const)],
        out_specs=out_specs,
        out_shape=out_shape,
        compiler_params=_params(("parallel", "parallel")),
        name="in_proj",
    )(x, pos3, mod3, norm_w.reshape(1, d), w_in_t, inv64, inv32, qnw, kvnw, krw)


def _log_sigmoid(v):
    return jnp.minimum(v, 0.0) - jnp.log(1.0 + jnp.exp(-jnp.abs(v)))


def _retention_kernel(lf_ref, lb_ref, q_ref, k_ref, v_ref, g_ref, gnw_ref, o_ref,
                      kv_ref, st_ref, *, chunk, n_chunks):
    hd = pl.program_id(1)
    c = chunk
    lgf = _log_sigmoid(jnp.full((1, 1), lf_ref[hd], _F32))
    lgb = _log_sigmoid(jnp.full((1, 1), lb_ref[hd], _F32))

    pos_l = lax.broadcasted_iota(jnp.int32, (1, c), 1).astype(_F32)
    kdec_f = jnp.exp(lgf * (c - 1.0 - pos_l))
    kdec_b = jnp.exp(lgb * pos_l)
    qdec_f = jnp.exp(lgf * (pos_l + 1.0))
    qdec_b = jnp.exp(lgb * (c - pos_l))
    jj = lax.broadcasted_iota(jnp.int32, (c, c), 0).astype(_F32)
    ii = lax.broadcasted_iota(jnp.int32, (c, c), 1).astype(_F32)
    dmat = jnp.where(ii >= jj, jnp.exp(lgf * jnp.maximum(ii - jj, 0.0)),
                     jnp.exp(lgb * jnp.maximum(jj - ii, 0.0)))
    lane = lax.broadcasted_iota(jnp.int32, (1, 2 * RET_QK), 1)
    is_f = lane < RET_QK
    sdec = jnp.where(is_f, jnp.exp(lgf * c), jnp.exp(lgb * c))

    def chunk_ds(i):
        return pl.ds(pl.multiple_of(i * c, c), c)

    def kv_body(i, carry):
        ds = chunk_ds(i)
        kc = k_ref[0, :, ds].astype(_F32)
        kfb = jnp.concatenate([kc * kdec_f, kc * kdec_b], axis=0).astype(_BF16)
        kv_ref[i] = lax.dot_general(v_ref[0, :, ds], kfb, _NT, preferred_element_type=_F32)
        return carry

    lax.fori_loop(0, n_chunks, kv_body, 0, unroll=8)

    def bwd_body(t, sb):
        i = n_chunks - 1 - t
        st_ref[i] = sb
        return sb * sdec + kv_ref[i]

    lax.fori_loop(0, n_chunks, bwd_body, jnp.zeros((RET_V, 2 * RET_QK), _F32))

    gnw = gnw_ref[0]

    def out_body(i, sf):
        ds = chunk_ds(i)
        qc = q_ref[0, :, ds]
        kc = k_ref[0, :, ds]
        st = lax.dot_general(kc, qc, _TN, preferred_element_type=_F32)
        pt = (st * dmat).astype(_BF16)
        qf = qc.astype(_F32)
        qfb = jnp.concatenate([qf * qdec_f, qf * qdec_b], axis=0).astype(_BF16)
        state = jnp.where(is_f, sf, st_ref[i]).astype(_BF16)
        o = (jnp.dot(v_ref[0, :, ds], pt, preferred_element_type=_F32)
             + jnp.dot(state, qfb, preferred_element_type=_F32))
        mu = jnp.mean(o, axis=0, keepdims=True)
        dlt = o - mu
        var = jnp.mean(dlt * dlt, axis=0, keepdims=True)
        on = dlt * lax.rsqrt(var + EPS) * gnw
        o_ref[0, :, ds] = (on * g_ref[0, :, ds].astype(_F32)).astype(_BF16)
        return sf * sdec + kv_ref[i]

    lax.fori_loop(0, n_chunks, out_body, jnp.zeros((RET_V, 2 * RET_QK), _F32), unroll=8)


def _retention(lf, lb, rq_t, rk_t, rv_t, rg_t, gnw3, chunk):
    bsz, _, s = rq_t.shape
    n_chunks = s // chunk
    smem = pl.BlockSpec(memory_space=pltpu.SMEM)
    return pl.pallas_call(
        functools.partial(_retention_kernel, chunk=chunk, n_chunks=n_chunks),
        grid=(bsz, RET_HEADS),
        in_specs=[smem, smem,
                  pl.BlockSpec((1, RET_QK, s), lambda b, h: (b, h, 0)),
                  pl.BlockSpec((1, RET_QK, s), lambda b, h: (b, h, 0)),
                  pl.BlockSpec((1, RET_V, s), lambda b, h: (b, h, 0)),
                  pl.BlockSpec((1, RET_V, s), lambda b, h: (b, h, 0)),
                  pl.BlockSpec((1, RET_V, 1), lambda b, h: (h, 0, 0))],
        out_specs=pl.BlockSpec((1, RET_V, s), lambda b, h: (b, h, 0)),
        out_shape=jax.ShapeDtypeStruct((bsz, RET_HEADS * RET_V, s), _BF16),
        scratch_shapes=[pltpu.VMEM((n_chunks, RET_V, 2 * RET_QK), _F32),
                        pltpu.VMEM((n_chunks, RET_V, 2 * RET_QK), _F32)],
        compiler_params=_params(("parallel", "parallel")),
        name="retention",
    )(lf, lb, rq_t, rk_t, rv_t, rg_t, gnw3)


def _mla_up_kernel(cq_ref, ckv_ref, kr_ref, cos_ref, sin_ref, wq_ref, wkv_ref,
                   qnw_ref, qrw_ref, knw_ref, qt_ref, k_ref, vt_ref, *, q_mul):
    tm = cq_ref.shape[2]
    q_all = jnp.dot(wq_ref[...], cq_ref[0], preferred_element_type=_F32)
    kv_all = jnp.dot(wkv_ref[...], ckv_ref[0], preferred_element_type=_F32)
    cos, sin = cos_ref[0], sin_ref[0]
    kr = kr_ref[0].astype(_F32)
    zpad = jnp.zeros((HEAD_PAD - MLA_NOPE - MLA_ROPE, tm), _F32)
    hq = MLA_NOPE + MLA_ROPE
    hkv = MLA_NOPE + MLA_V
    r2 = MLA_ROPE // 2
    for hd in range(MLA_HEADS):
        qn = q_all[hd * hq: hd * hq + MLA_NOPE]
        qr = q_all[hd * hq + MLA_NOPE: (hd + 1) * hq]
        qn = qn * lax.rsqrt(jnp.mean(qn * qn, axis=0, keepdims=True) + EPS) * qnw_ref[...]
        qr = qr * lax.rsqrt(jnp.mean(qr * qr, axis=0, keepdims=True) + EPS) * qrw_ref[...]
        q1, q2 = qr[:r2], qr[r2:]
        qh = jnp.concatenate([qn, q1 * cos - q2 * sin, q2 * cos + q1 * sin, zpad], axis=0)
        qt_ref[0, hd] = (qh * q_mul).astype(_BF16)
        kn = kv_all[hd * hkv: hd * hkv + MLA_NOPE]
        kn = kn * lax.rsqrt(jnp.mean(kn * kn, axis=0, keepdims=True) + EPS) * knw_ref[...]
        kh = jnp.concatenate([kn, kr, zpad], axis=0)
        k_ref[0, hd] = kh.T.astype(_BF16)
        vt_ref[0, hd] = kv_all[hd * hkv + MLA_NOPE: (hd + 1) * hkv].astype(_BF16)


def _mla_up(cq_t, ckv_t, kr_t, cos_t, sin_t, wq_t, wkv_t, qnw, qrw, knw, tm):
    bsz, _, s = cq_t.shape
    tok = lambda b, i: (b, 0, i)
    const = lambda b, i: (0, 0)
    q_mul = (MLA_NOPE + MLA_ROPE) ** -0.5 * LOG2E
    return pl.pallas_call(
        functools.partial(_mla_up_kernel, q_mul=q_mul),
        grid=(bsz, s // tm),
        in_specs=[pl.BlockSpec((1, Q_LORA, tm), tok),
                  pl.BlockSpec((1, KV_LORA, tm), tok),
                  pl.BlockSpec((1, MLA_ROPE, tm), tok),
                  pl.BlockSpec((1, MLA_ROPE // 2, tm), tok),
                  pl.BlockSpec((1, MLA_ROPE // 2, tm), tok),
                  pl.BlockSpec(wq_t.shape, const),
                  pl.BlockSpec(wkv_t.shape, const),
                  pl.BlockSpec(qnw.shape, const),
                  pl.BlockSpec(qrw.shape, const),
                  pl.BlockSpec(knw.shape, const)],
        out_specs=[pl.BlockSpec((1, MLA_HEADS, HEAD_PAD, tm), lambda b, i: (b, 0, 0, i)),
                   pl.BlockSpec((1, MLA_HEADS, tm, HEAD_PAD), lambda b, i: (b, 0, i, 0)),
                   pl.BlockSpec((1, MLA_HEADS, MLA_V, tm), lambda b, i: (b, 0, 0, i))],
        out_shape=[jax.ShapeDtypeStruct((bsz, MLA_HEADS, HEAD_PAD, s), _BF16),
                   jax.ShapeDtypeStruct((bsz, MLA_HEADS, s, HEAD_PAD), _BF16),
                   jax.ShapeDtypeStruct((bsz, MLA_HEADS, MLA_V, s), _BF16)],
        compiler_params=_params(("parallel", "parallel")),
        name="mla_up",
    )(cq_t, ckv_t, kr_t, cos_t, sin_t, wq_t, wkv_t, qnw, qrw, knw)


def _attn_kernel(qt_ref, k_ref, vt_ref, o_ref, *bufs, tq, tk, n_q, n_kv):
    nb = len(bufs)

    def q_ds(qi):
        return pl.ds(pl.multiple_of(qi * tq, tq), tq)

    def kv_ds(j):
        return pl.ds(pl.multiple_of(j * tk, tk), tk)

    def scores(qi, j, m, s_ref):
        st = jnp.dot(k_ref[0, 0, kv_ds(j), :], qt_ref[0, 0, :, q_ds(qi)],
                     preferred_element_type=_F32)
        s_ref[...] = st
        return jnp.maximum(m, jnp.max(st, axis=0, keepdims=True))

    def accumulate(j, m_prev, m, l, acc, s_ref):
        alpha = jnp.exp2(m_prev - m)
        p = jnp.exp2(s_ref[...] - m)
        l = alpha * l + jnp.sum(p, axis=0, keepdims=True)
        acc = alpha * acc + jnp.dot(vt_ref[0, 0, :, kv_ds(j)], p.astype(_BF16),
                                    preferred_element_type=_F32)
        return l, acc

    neg_inf = jnp.full((1, tq), -jnp.inf, _F32)

    def body(qi, carry):
        ms = [neg_inf] + list(carry)
        nxt = []
        qn = jnp.minimum(qi + 1, n_q - 1)
        l = jnp.zeros((1, tq), _F32)
        acc = jnp.zeros((MLA_V, tq), _F32)
        for j in range(n_kv):
            t = j + ATTN_AHEAD
            if t < n_kv:
                ms.append(scores(qi, t, ms[-1], bufs[t % nb]))
            else:
                nxt.append(scores(qn, t - n_kv, nxt[-1] if nxt else neg_inf, bufs[t % nb]))
            l, acc = accumulate(j, ms[j], ms[j + 1], l, acc, bufs[j % nb])
        o_ref[0, :, q_ds(qi)] = (acc * (1.0 / l)).astype(_BF16)
        return tuple(nxt)

    first = []
    for j in range(ATTN_AHEAD):
        first.append(scores(0, j, first[-1] if first else neg_inf, bufs[j]))
    lax.fori_loop(0, n_q, body, tuple(first))


ATTN_AHEAD = 2
ATTN_BUFS = 4


def _attention(qt, k, vt, tq, tk):
    bsz, nh, _, s = qt.shape
    n_kv = s // tk
    assert s % tq == 0 and s % tk == 0 and n_kv % ATTN_BUFS == 0 and ATTN_AHEAD < ATTN_BUFS
    return pl.pallas_call(
        functools.partial(_attn_kernel, tq=tq, tk=tk, n_q=s // tq, n_kv=s // tk),
        grid=(bsz, nh),
        in_specs=[pl.BlockSpec((1, 1, HEAD_PAD, s), lambda b, h: (b, h, 0, 0)),
                  pl.BlockSpec((1, 1, s, HEAD_PAD), lambda b, h: (b, h, 0, 0)),
                  pl.BlockSpec((1, 1, MLA_V, s), lambda b, h: (b, h, 0, 0))],
        out_specs=pl.BlockSpec((1, MLA_V, s), lambda b, h: (b, h, 0)),
        out_shape=jax.ShapeDtypeStruct((bsz, nh * MLA_V, s), _BF16),
        scratch_shapes=[pltpu.VMEM((tk, tq), _F32)] * ATTN_BUFS,
        compiler_params=_params(("parallel", "parallel")),
        name="mla_attention",
    )(qt, k, vt)


def _out_kernel(x_ref, ret_ref, att_ref, mg_ref, mod_ref, onw_ref, w_ref, o_ref):
    att = att_ref[0].astype(_F32)
    mla = att * lax.rsqrt(jnp.mean(att * att, axis=0, keepdims=True) + EPS) * onw_ref[...]
    mla = (mla * mg_ref[0].astype(_F32)).astype(_BF16)
    cat = jnp.concatenate([ret_ref[0], mla], axis=0)
    yt = jnp.dot(w_ref[...], cat, preferred_element_type=_F32)
    gate = mod_ref[0, 2:3, :]
    o_ref[0] = x_ref[0] + gate * yt.T


def _out_proj(x, ret_t, att_t, mg_t, mod3, onw, w_out_t, tm):
    bsz, s, d = x.shape
    tok = lambda b, i: (b, 0, i)
    const = lambda b, i: (0, 0)
    return pl.pallas_call(
        _out_kernel,
        grid=(bsz, s // tm),
        in_specs=[pl.BlockSpec((1, tm, d), lambda b, i: (b, i, 0)),
                  pl.BlockSpec((1, ret_t.shape[1], tm), tok),
                  pl.BlockSpec((1, att_t.shape[1], tm), tok),
                  pl.BlockSpec((1, mg_t.shape[1], tm), tok),
                  pl.BlockSpec((1, 3, d), lambda b, i: (b, 0, 0)),
                  pl.BlockSpec(onw.shape, const),
                  pl.BlockSpec(w_out_t.shape, const)],
        out_specs=pl.BlockSpec((1, tm, d), lambda b, i: (b, i, 0)),
        out_shape=jax.ShapeDtypeStruct((bsz, s, d), x.dtype),
        compiler_params=_params(("parallel", "parallel")),
        name="out_proj",
    )(x, ret_t, att_t, mg_t, mod3, onw, w_out_t)


def _col(v):
    return v.astype(_F32).reshape(-1, 1)


def _block_sizes(s):
    tm = min(512, s)
    tq = min(512, s)
    tk = min(1024, s // ATTN_BUFS)
    chunk = min(256, s)
    return tm, tq, tk, chunk


def kernel(x, c, positions, norm_w, w_ada, b_ada, w_in, ret_decay_logit_fwd, ret_decay_logit_bwd, ret_gn_w, q_norm_w, w_uq, kv_norm_w, w_ukv, qn_nope_w, qn_rope_w, kn_nope_w, kn_rope_w, mla_out_norm_w, w_out):
    bsz, s, d = x.shape
    tm, tq, tk, chunk = _block_sizes(s)

    w_in_t = w_in.T.astype(_BF16)
    wq_t = w_uq.T.astype(_BF16)
    wkv_t = w_ukv.T.astype(_BF16)
    w_out_t = w_out.T.astype(_BF16)
    inv64 = (ROPE_BASE ** (-jnp.arange(0, RET_QK, 2, dtype=_F32) / RET_QK)).reshape(-1, 1)
    inv32 = (ROPE_BASE ** (-jnp.arange(0, MLA_ROPE, 2, dtype=_F32) / MLA_ROPE)).reshape(-1, 1)

    mod3 = _ada(c, w_ada, b_ada).reshape(bsz, 3, d)
    pos3 = positions.reshape(bsz, 1, s)

    (rq_t, rk_t, rv_t, rg_t, cq_t, ckv_t, kr_t, mg_t, cos_t, sin_t) = _inproj(
        x, pos3, mod3, norm_w, w_in_t, inv64, inv32,
        _col(q_norm_w), _col(kv_norm_w), _col(kn_rope_w), tm)

    ret_t = _retention(ret_decay_logit_fwd.astype(_F32), ret_decay_logit_bwd.astype(_F32),
                       rq_t, rk_t, rv_t, rg_t,
                       ret_gn_w.astype(_F32).reshape(RET_HEADS, RET_V, 1), chunk)

    qt, k, vt = _mla_up(cq_t, ckv_t, kr_t, cos_t, sin_t, wq_t, wkv_t,
                        _col(qn_nope_w), _col(qn_rope_w), _col(kn_nope_w), tm)
    att_t = _attention(qt, k, vt, tq, tk)

    return _out_proj(x, ret_t, att_t, mg_t, mod3, _col(mla_out_norm_w), w_out_t, tm)
```

```python
import functools
import math

import jax
import jax.numpy as jnp
from jax import lax
from jax.experimental import pallas as pl
from jax.experimental.pallas import tpu as pltpu

RET_HEADS = 8
RET_QK = 64
RET_V = 128
MLA_HEADS = 8
MLA_NOPE = 64
MLA_ROPE = 32
MLA_V = 128
Q_LORA = 384
KV_LORA = 256
ROPE_BASE = 10000.0
EPS = 1e-6
HEAD_PAD = 128
LOG2E = 1.4426950408889634

_SIZES = (RET_HEADS * RET_QK, RET_HEADS * RET_QK, RET_HEADS * RET_V, RET_HEADS * RET_V,
          Q_LORA, KV_LORA, MLA_ROPE, MLA_HEADS * MLA_V)
_OFFS = tuple(sum(_SIZES[:i]) for i in range(len(_SIZES) + 1))

VMEM_LIMIT = 56 * 1024 * 1024

_NT = (((1,), (1,)), ((), ()))
_TN = (((0,), (0,)), ((), ()))
_F32 = jnp.float32
_BF16 = jnp.bfloat16


def _silu(v):
    return v * (0.5 + 0.5 * jnp.tanh(0.5 * v))


def _params(sem):
    return pltpu.CompilerParams(dimension_semantics=sem, vmem_limit_bytes=VMEM_LIMIT)


def _ada_kernel(c_ref, w_ref, b_ref, o_ref):
    a = _silu(c_ref[...]).astype(_BF16)
    o_ref[...] = jnp.dot(a, w_ref[...].astype(_BF16), preferred_element_type=_F32) + b_ref[...]


def _ada(c, w_ada, b_ada):
    bsz, d = c.shape
    n = w_ada.shape[1]
    tn = 1024
    return pl.pallas_call(
        _ada_kernel,
        grid=(n // tn,),
        in_specs=[pl.BlockSpec((bsz, d), lambda j: (0, 0)),
                  pl.BlockSpec((d, tn), lambda j: (0, j)),
                  pl.BlockSpec((1, tn), lambda j: (0, j))],
        out_specs=pl.BlockSpec((bsz, tn), lambda j: (0, j)),
        out_shape=jax.ShapeDtypeStruct((bsz, n), _F32),
        compiler_params=_params(("arbitrary",)),
        name="ada_mod",
    )(c, w_ada, b_ada.reshape(1, n))


def _inproj_kernel(x_ref, pos_ref, mod_ref, nw_ref, w_ref, inv64_ref, inv32_ref,
                   qnw_ref, kvnw_ref, krw_ref,
                   rq_ref, rk_ref, rv_ref, rg_ref, cq_ref, ckv_ref, kr_ref, mg_ref,
                   cos_ref, sin_ref):
    x = x_ref[0]
    shift = mod_ref[0, 0:1, :]
    scale = mod_ref[0, 1:2, :]
    y = x * lax.rsqrt(jnp.mean(x * x, axis=-1, keepdims=True) + EPS) * nw_ref[...]
    h = (y * (1.0 + scale) + shift).astype(_BF16)

    def proj(i):
        return lax.dot_general(w_ref[_OFFS[i]:_OFFS[i + 1], :], h, _NT,
                               preferred_element_type=_F32)

    pos = pos_ref[0].astype(_F32)
    ang64 = inv64_ref[...] * pos
    cos64, sin64 = jnp.cos(ang64), jnp.sin(ang64)
    ang32 = inv32_ref[...] * pos
    cos32, sin32 = jnp.cos(ang32), jnp.sin(ang32)
    cos_ref[0] = cos32
    sin_ref[0] = sin32

    rg_ref[0] = _silu(proj(3)).astype(_BF16)
    mg_ref[0] = _silu(proj(7)).astype(_BF16)

    half = RET_QK // 2
    for src, dst, mul in ((0, rq_ref, 1.0), (1, rk_ref, RET_QK ** -0.5)):
        p = proj(src)
        for hd in range(RET_HEADS):
            x1 = p[hd * RET_QK: hd * RET_QK + half]
            x2 = p[hd * RET_QK + half: (hd + 1) * RET_QK]
            dst[0, hd * RET_QK: hd * RET_QK + half, :] = ((x1 * cos64 - x2 * sin64) * mul).astype(_BF16)
            dst[0, hd * RET_QK + half: (hd + 1) * RET_QK, :] = ((x2 * cos64 + x1 * sin64) * mul).astype(_BF16)

    cq = proj(4)
    cq_ref[0] = (cq * lax.rsqrt(jnp.mean(cq * cq, axis=0, keepdims=True) + EPS) * qnw_ref[...]).astype(_BF16)
    ckv = proj(5)
    ckv_ref[0] = (ckv * lax.rsqrt(jnp.mean(ckv * ckv, axis=0, keepdims=True) + EPS) * kvnw_ref[...]).astype(_BF16)

    kr = proj(6)
    kr = kr * lax.rsqrt(jnp.mean(kr * kr, axis=0, keepdims=True) + EPS) * krw_ref[...]
    k1, k2 = kr[:MLA_ROPE // 2], kr[MLA_ROPE // 2:]
    kr_ref[0, :MLA_ROPE // 2, :] = (k1 * cos32 - k2 * sin32).astype(_BF16)
    kr_ref[0, MLA_ROPE // 2:, :] = (k2 * cos32 + k1 * sin32).astype(_BF16)

    rv_ref[0] = proj(2).astype(_BF16)


def _inproj(x, pos3, mod3, norm_w, w_in_t, inv64, inv32, qnw, kvnw, krw, tm):
    bsz, s, d = x.shape
    grid = (bsz, s // tm)
    const = lambda b, i: (0, 0)
    tok = lambda b, i: (b, 0, i)
    sizes = (_SIZES[0], _SIZES[1], _SIZES[2], _SIZES[3], _SIZES[4], _SIZES[5], _SIZES[6], _SIZES[7])
    out_shape = [jax.ShapeDtypeStruct((bsz, n, s), _BF16) for n in sizes]
    out_shape += [jax.ShapeDtypeStruct((bsz, MLA_ROPE // 2, s), _F32)] * 2
    out_specs = [pl.BlockSpec((1, n, tm), tok) for n in sizes]
    out_specs += [pl.BlockSpec((1, MLA_ROPE // 2, tm), tok)] * 2
    return pl.pallas_call(
        _inproj_kernel,
        grid=grid,
        in_specs=[pl.BlockSpec((1, tm, d), lambda b, i: (b, i, 0)),
                  pl.BlockSpec((1, 1, tm), tok),
                  pl.BlockSpec((1, 3, d), lambda b, i: (b, 0, 0)),
                  pl.BlockSpec((1, d), const),
                  pl.BlockSpec(w_in_t.shape, const),
                  pl.BlockSpec(inv64.shape, const),
                  pl.BlockSpec(inv32.shape, const),
                  pl.BlockSpec(qnw.shape, const),
                  pl.BlockSpec(kvnw.shape, const),
                  pl.BlockSpec(krw.shape, const)],
        out_specs=out_specs,
        out_shape=out_shape,
        compiler_params=_params(("parallel", "parallel")),
        name="in_proj",
    )(x, pos3, mod3, norm_w.reshape(1, d), w_in_t, inv64, inv32, qnw, kvnw, krw)


def _log_sigmoid(v):
    return jnp.minimum(v, 0.0) - jnp.log(1.0 + jnp.exp(-jnp.abs(v)))


def _retention_kernel(lf_ref, lb_ref, q_ref, k_ref, v_ref, g_ref, gnw_ref, o_ref,
                      kv_ref, st_ref, *, chunk, n_chunks):
    hd = pl.program_id(1)
    c = chunk
    lgf = _log_sigmoid(jnp.full((1, 1), lf_ref[hd], _F32))
    lgb = _log_sigmoid(jnp.full((1, 1), lb_ref[hd], _F32))

    pos_l = lax.broadcasted_iota(jnp.int32, (1, c), 1).astype(_F32)
    kdec_f = jnp.exp(lgf * (c - 1.0 - pos_l))
    kdec_b = jnp.exp(lgb * pos_l)
    qdec_f = jnp.exp(lgf * (pos_l + 1.0))
    qdec_b = jnp.exp(lgb * (c - pos_l))
    jj = lax.broadcasted_iota(jnp.int32, (c, c), 0).astype(_F32)
    ii = lax.broadcasted_iota(jnp.int32, (c, c), 1).astype(_F32)
    dmat = jnp.where(ii >= jj, jnp.exp(lgf * jnp.maximum(ii - jj, 0.0)),
                     jnp.exp(lgb * jnp.maximum(jj - ii, 0.0)))
    lane = lax.broadcasted_iota(jnp.int32, (1, 2 * RET_QK), 1)
    is_f = lane < RET_QK
    sdec = jnp.where(is_f, jnp.exp(lgf * c), jnp.exp(lgb * c))

    def chunk_ds(i):
        return pl.ds(pl.multiple_of(i * c, c), c)

    def kv_body(i, carry):
        ds = chunk_ds(i)
        kc = k_ref[0, :, ds].astype(_F32)
        kfb = jnp.concatenate([kc * kdec_f, kc * kdec_b], axis=0).astype(_BF16)
        kv_ref[i] = lax.dot_general(v_ref[0, :, ds], kfb, _NT, preferred_element_type=_F32)
        return carry

    lax.fori_loop(0, n_chunks, kv_body, 0, unroll=8)

    def bwd_body(t, sb):
        i = n_chunks - 1 - t
        st_ref[i] = sb
        return sb * sdec + kv_ref[i]

    lax.fori_loop(0, n_chunks, bwd_body, jnp.zeros((RET_V, 2 * RET_QK), _F32))

    gnw = gnw_ref[0]

    def out_body(i, sf):
        ds = chunk_ds(i)
        qc = q_ref[0, :, ds]
        kc = k_ref[0, :, ds]
        st = lax.dot_general(kc, qc, _TN, preferred_element_type=_F32)
        pt = (st * dmat).astype(_BF16)
        qf = qc.astype(_F32)
        qfb = jnp.concatenate([qf * qdec_f, qf * qdec_b], axis=0).astype(_BF16)
        state = jnp.where(is_f, sf, st_ref[i]).astype(_BF16)
        o = (jnp.dot(v_ref[0, :, ds], pt, preferred_element_type=_F32)
             + jnp.dot(state, qfb, preferred_element_type=_F32))
        mu = jnp.mean(o, axis=0, keepdims=True)
        dlt = o - mu
        var = jnp.mean(dlt * dlt, axis=0, keepdims=True)
        on = dlt * lax.rsqrt(var + EPS) * gnw
        o_ref[0, :, ds] = (on * g_ref[0, :, ds].astype(_F32)).astype(_BF16)
        return sf * sdec + kv_ref[i]

    lax.fori_loop(0, n_chunks, out_body, jnp.zeros((RET_V, 2 * RET_QK), _F32), unroll=8)


def _retention(lf, lb, rq_t, rk_t, rv_t, rg_t, gnw3, chunk):
    bsz, _, s = rq_t.shape
    n_chunks = s // chunk
    smem = pl.BlockSpec(memory_space=pltpu.SMEM)
    return pl.pallas_call(
        functools.partial(_retention_kernel, chunk=chunk, n_chunks=n_chunks),
        grid=(bsz, RET_HEADS),
        in_specs=[smem, smem,
                  pl.BlockSpec((1, RET_QK, s), lambda b, h: (b, h, 0)),
                  pl.BlockSpec((1, RET_QK, s), lambda b, h: (b, h, 0)),
                  pl.BlockSpec((1, RET_V, s), lambda b, h: (b, h, 0)),
                  pl.BlockSpec((1, RET_V, s), lambda b, h: (b, h, 0)),
                  pl.BlockSpec((1, RET_V, 1), lambda b, h: (h, 0, 0))],
        out_specs=pl.BlockSpec((1, RET_V, s), lambda b, h: (b, h, 0)),
        out_shape=jax.ShapeDtypeStruct((bsz, RET_HEADS * RET_V, s), _BF16),
        scratch_shapes=[pltpu.VMEM((n_chunks, RET_V, 2 * RET_QK), _F32),
                        pltpu.VMEM((n_chunks, RET_V, 2 * RET_QK), _F32)],
        compiler_params=_params(("parallel", "parallel")),
        name="retention",
    )(lf, lb, rq_t, rk_t, rv_t, rg_t, gnw3)


def _max_sq_norm(v):
    return jnp.max(jnp.sum(v * v, axis=0, keepdims=True), axis=1, keepdims=True)


def _mla_up_kernel(cq_ref, ckv_ref, kr_ref, cos_ref, sin_ref, wq_ref, wkv_ref,
                   qnw_ref, qrw_ref, knw_ref, qt_ref, k_ref, vt_ref, qn2_ref, kn2_ref, *, q_mul):
    tm = cq_ref.shape[2]
    q_all = jnp.dot(wq_ref[...], cq_ref[0], preferred_element_type=_F32)
    kv_all = jnp.dot(wkv_ref[...], ckv_ref[0], preferred_element_type=_F32)
    cos, sin = cos_ref[0], sin_ref[0]
    kr = kr_ref[0].astype(_F32)
    zpad = jnp.zeros((HEAD_PAD - MLA_NOPE - MLA_ROPE, tm), _F32)
    hq = MLA_NOPE + MLA_ROPE
    hkv = MLA_NOPE + MLA_V
    r2 = MLA_ROPE // 2
    for hd in range(MLA_HEADS):
        qn = q_all[hd * hq: hd * hq + MLA_NOPE]
        qr = q_all[hd * hq + MLA_NOPE: (hd + 1) * hq]
        qn = qn * lax.rsqrt(jnp.mean(qn * qn, axis=0, keepdims=True) + EPS) * qnw_ref[...]
        qr = qr * lax.rsqrt(jnp.mean(qr * qr, axis=0, keepdims=True) + EPS) * qrw_ref[...]
        q1, q2 = qr[:r2], qr[r2:]
        qh = jnp.concatenate([qn, q1 * cos - q2 * sin, q2 * cos + q1 * sin, zpad], axis=0)
        qh = qh * q_mul
        qt_ref[0, hd] = qh.astype(_BF16)
        qn2_ref[0, 0, hd:hd + 1, :] = jnp.broadcast_to(_max_sq_norm(qh), (1, HEAD_PAD))
        kn = kv_all[hd * hkv: hd * hkv + MLA_NOPE]
        kn = kn * lax.rsqrt(jnp.mean(kn * kn, axis=0, keepdims=True) + EPS) * knw_ref[...]
        kh = jnp.concatenate([kn, kr, zpad], axis=0)
        k_ref[0, hd] = kh.T.astype(_BF16)
        kn2_ref[0, 0, hd:hd + 1, :] = jnp.broadcast_to(_max_sq_norm(kh), (1, HEAD_PAD))
        vt_ref[0, hd] = kv_all[hd * hkv + MLA_NOPE: (hd + 1) * hkv].astype(_BF16)


def _mla_up(cq_t, ckv_t, kr_t, cos_t, sin_t, wq_t, wkv_t, qnw, qrw, knw, tm):
    bsz, _, s = cq_t.shape
    tok = lambda b, i: (b, 0, i)
    const = lambda b, i: (0, 0)
    q_mul = (MLA_NOPE + MLA_ROPE) ** -0.5 * LOG2E
    return pl.pallas_call(
        functools.partial(_mla_up_kernel, q_mul=q_mul),
        grid=(bsz, s // tm),
        in_specs=[pl.BlockSpec((1, Q_LORA, tm), tok),
                  pl.BlockSpec((1, KV_LORA, tm), tok),
                  pl.BlockSpec((1, MLA_ROPE, tm), tok),
                  pl.BlockSpec((1, MLA_ROPE // 2, tm), tok),
                  pl.BlockSpec((1, MLA_ROPE // 2, tm), tok),
                  pl.BlockSpec(wq_t.shape, const),
                  pl.BlockSpec(wkv_t.shape, const),
                  pl.BlockSpec(qnw.shape, const),
                  pl.BlockSpec(qrw.shape, const),
                  pl.BlockSpec(knw.shape, const)],
        out_specs=[pl.BlockSpec((1, MLA_HEADS, HEAD_PAD, tm), lambda b, i: (b, 0, 0, i)),
                   pl.BlockSpec((1, MLA_HEADS, tm, HEAD_PAD), lambda b, i: (b, 0, i, 0)),
                   pl.BlockSpec((1, MLA_HEADS, MLA_V, tm), lambda b, i: (b, 0, 0, i)),
                   pl.BlockSpec((1, 1, MLA_HEADS, HEAD_PAD), lambda b, i: (b, i, 0, 0)),
                   pl.BlockSpec((1, 1, MLA_HEADS, HEAD_PAD), lambda b, i: (b, i, 0, 0))],
        out_shape=[jax.ShapeDtypeStruct((bsz, MLA_HEADS, HEAD_PAD, s), _BF16),
                   jax.ShapeDtypeStruct((bsz, MLA_HEADS, s, HEAD_PAD), _BF16),
                   jax.ShapeDtypeStruct((bsz, MLA_HEADS, MLA_V, s), _BF16),
                   jax.ShapeDtypeStruct((bsz, s // tm, MLA_HEADS, HEAD_PAD), _F32),
                   jax.ShapeDtypeStruct((bsz, s // tm, MLA_HEADS, HEAD_PAD), _F32)],
        compiler_params=_params(("parallel", "parallel")),
        name="mla_up",
    )(cq_t, ckv_t, kr_t, cos_t, sin_t, wq_t, wkv_t, qnw, qrw, knw)


def _attn_kernel(fast_ref, qt_ref, k_ref, vt_ref, o_ref, *bufs, tq, tk, n_q, n_kv):
    nb = len(bufs)
    fast = fast_ref[pl.program_id(0) * pl.num_programs(1) + pl.program_id(1)]

    def q_ds(qi):
        return pl.ds(pl.multiple_of(qi * tq, tq), tq)

    def kv_ds(j):
        return pl.ds(pl.multiple_of(j * tk, tk), tk)

    def write(qi, l, acc):
        o_ref[0, :, q_ds(qi)] = (acc * (1.0 / l)).astype(_BF16)

    @pl.when(fast != 0)
    def _fast_path():
        def body(qi, carry):
            qt = qt_ref[0, 0, :, q_ds(qi)]
            l = jnp.zeros((1, tq), _F32)
            acc = jnp.zeros((MLA_V, tq), _F32)
            for j in range(n_kv):
                st = jnp.dot(k_ref[0, 0, kv_ds(j), :], qt, preferred_element_type=_F32)
                p = jnp.exp2(st)
                l = l + jnp.sum(p, axis=0, keepdims=True)
                acc = acc + jnp.dot(vt_ref[0, 0, :, kv_ds(j)], p.astype(_BF16),
                                    preferred_element_type=_F32)
            write(qi, l, acc)
            return carry

        lax.fori_loop(0, n_q, body, 0)

    @pl.when(fast == 0)
    def _safe_path():
        def scores(qi, j, m, s_ref):
            st = jnp.dot(k_ref[0, 0, kv_ds(j), :], qt_ref[0, 0, :, q_ds(qi)],
                         preferred_element_type=_F32)
            s_ref[...] = st
            return jnp.maximum(m, jnp.max(st, axis=0, keepdims=True))

        def accumulate(j, m_prev, m, l, acc, s_ref):
            alpha = jnp.exp2(m_prev - m)
            p = jnp.exp2(s_ref[...] - m)
            l = alpha * l + jnp.sum(p, axis=0, keepdims=True)
            acc = alpha * acc + jnp.dot(vt_ref[0, 0, :, kv_ds(j)], p.astype(_BF16),
                                        preferred_element_type=_F32)
            return l, acc

        neg_inf = jnp.full((1, tq), -jnp.inf, _F32)

        def body(qi, carry):
            ms = [neg_inf] + list(carry)
            nxt = []
            qn = jnp.minimum(qi + 1, n_q - 1)
            l = jnp.zeros((1, tq), _F32)
            acc = jnp.zeros((MLA_V, tq), _F32)
            for j in range(n_kv):
                t = j + ATTN_AHEAD
                if t < n_kv:
                    ms.append(scores(qi, t, ms[-1], bufs[t % nb]))
                else:
                    nxt.append(scores(qn, t - n_kv, nxt[-1] if nxt else neg_inf, bufs[t % nb]))
                l, acc = accumulate(j, ms[j], ms[j + 1], l, acc, bufs[j % nb])
            write(qi, l, acc)
            return tuple(nxt)

        first = []
        for j in range(ATTN_AHEAD):
            first.append(scores(0, j, first[-1] if first else neg_inf, bufs[j]))
        lax.fori_loop(0, n_q, body, tuple(first))


ATTN_AHEAD = 2
ATTN_BUFS = 4
SCORE_BOUND_LOG2 = 60.0


def _attention(fast, qt, k, vt, tq, tk):
    bsz, nh, _, s = qt.shape
    n_kv = s // tk
    assert s % tq == 0 and s % tk == 0 and n_kv % ATTN_BUFS == 0 and ATTN_AHEAD < ATTN_BUFS
    return pl.pallas_call(
        functools.partial(_attn_kernel, tq=tq, tk=tk, n_q=s // tq, n_kv=s // tk),
        grid=(bsz, nh),
        in_specs=[pl.BlockSpec(memory_space=pltpu.SMEM),
                  pl.BlockSpec((1, 1, HEAD_PAD, s), lambda b, h: (b, h, 0, 0)),
                  pl.BlockSpec((1, 1, s, HEAD_PAD), lambda b, h: (b, h, 0, 0)),
                  pl.BlockSpec((1, 1, MLA_V, s), lambda b, h: (b, h, 0, 0))],
        out_specs=pl.BlockSpec((1, MLA_V, s), lambda b, h: (b, h, 0)),
        out_shape=jax.ShapeDtypeStruct((bsz, nh * MLA_V, s), _BF16),
        scratch_shapes=[pltpu.VMEM((tk, tq), _F32)] * ATTN_BUFS,
        compiler_params=_params(("parallel", "parallel")),
        name="mla_attention",
    )(fast, qt, k, vt)


def _out_kernel(x_ref, ret_ref, att_ref, mg_ref, mod_ref, onw_ref, w_ref, o_ref):
    att = att_ref[0].astype(_F32)
    mla = att * lax.rsqrt(jnp.mean(att * att, axis=0, keepdims=True) + EPS) * onw_ref[...]
    mla = (mla * mg_ref[0].astype(_F32)).astype(_BF16)
    cat = jnp.concatenate([ret_ref[0], mla], axis=0)
    yt = jnp.dot(w_ref[...], cat, preferred_element_type=_F32)
    gate = mod_ref[0, 2:3, :]
    o_ref[0] = x_ref[0] + gate * yt.T


def _out_proj(x, ret_t, att_t, mg_t, mod3, onw, w_out_t, tm):
    bsz, s, d = x.shape
    tok = lambda b, i: (b, 0, i)
    const = lambda b, i: (0, 0)
    return pl.pallas_call(
        _out_kernel,
        grid=(bsz, s // tm),
        in_specs=[pl.BlockSpec((1, tm, d), lambda b, i: (b, i, 0)),
                  pl.BlockSpec((1, ret_t.shape[1], tm), tok),
                  pl.BlockSpec((1, att_t.shape[1], tm), tok),
                  pl.BlockSpec((1, mg_t.shape[1], tm), tok),
                  pl.BlockSpec((1, 3, d), lambda b, i: (b, 0, 0)),
                  pl.BlockSpec(onw.shape, const),
                  pl.BlockSpec(w_out_t.shape, const)],
        out_specs=pl.BlockSpec((1, tm, d), lambda b, i: (b, i, 0)),
        out_shape=jax.ShapeDtypeStruct((bsz, s, d), x.dtype),
        compiler_params=_params(("parallel", "parallel")),
        name="out_proj",
    )(x, ret_t, att_t, mg_t, mod3, onw, w_out_t)


def _score_bound_ok(qn2, kn2):
    bound2 = jnp.max(qn2, axis=(1, 3)) * jnp.max(kn2, axis=(1, 3))
    return (bound2 <= SCORE_BOUND_LOG2 ** 2).astype(jnp.int32).reshape(-1)


def _col(v):
    return v.astype(_F32).reshape(-1, 1)


def _block_sizes(s):
    tm = min(512, s)
    tq = min(512, s)
    tk = min(1024, s // ATTN_BUFS)
    chunk = min(256, s)
    return tm, tq, tk, chunk


def kernel(x, c, positions, norm_w, w_ada, b_ada, w_in, ret_decay_logit_fwd, ret_decay_logit_bwd, ret_gn_w, q_norm_w, w_uq, kv_norm_w, w_ukv, qn_nope_w, qn_rope_w, kn_nope_w, kn_rope_w, mla_out_norm_w, w_out):
    bsz, s, d = x.shape
    tm, tq, tk, chunk = _block_sizes(s)

    w_in_t = w_in.T.astype(_BF16)
    wq_t = w_uq.T.astype(_BF16)
    wkv_t = w_ukv.T.astype(_BF16)
    w_out_t = w_out.T.astype(_BF16)
    inv64 = (ROPE_BASE ** (-jnp.arange(0, RET_QK, 2, dtype=_F32) / RET_QK)).reshape(-1, 1)
    inv32 = (ROPE_BASE ** (-jnp.arange(0, MLA_ROPE, 2, dtype=_F32) / MLA_ROPE)).reshape(-1, 1)

    mod3 = _ada(c, w_ada, b_ada).reshape(bsz, 3, d)
    pos3 = positions.reshape(bsz, 1, s)

    (rq_t, rk_t, rv_t, rg_t, cq_t, ckv_t, kr_t, mg_t, cos_t, sin_t) = _inproj(
        x, pos3, mod3, norm_w, w_in_t, inv64, inv32,
        _col(q_norm_w), _col(kv_norm_w), _col(kn_rope_w), tm)

    ret_t = _retention(ret_decay_logit_fwd.astype(_F32), ret_decay_logit_bwd.astype(_F32),
                       rq_t, rk_t, rv_t, rg_t,
                       ret_gn_w.astype(_F32).reshape(RET_HEADS, RET_V, 1), chunk)

    qt, k, vt, qn2, kn2 = _mla_up(cq_t, ckv_t, kr_t, cos_t, sin_t, wq_t, wkv_t,
                                  _col(qn_nope_w), _col(qn_rope_w), _col(kn_nope_w), tm)
    att_t = _attention(_score_bound_ok(qn2, kn2), qt, k, vt, tq, tk)

    return _out_proj(x, ret_t, att_t, mg_t, mod3, _col(mla_out_norm_w), w_out_t, tm)
```

```python
import functools
import math

import jax
import jax.numpy as jnp
from jax import lax
from jax.experimental import pallas as pl
from jax.experimental.pallas import tpu as pltpu

RET_HEADS = 8
RET_QK = 64
RET_V = 128
MLA_HEADS = 8
MLA_NOPE = 64
MLA_ROPE = 32
MLA_V = 128
Q_LORA = 384
KV_LORA = 256
ROPE_BASE = 10000.0
EPS = 1e-6
HEAD_PAD = 128
LOG2E = 1.4426950408889634

_SIZES = (RET_HEADS * RET_QK, RET_HEADS * RET_QK, RET_HEADS * RET_V, RET_HEADS * RET_V,
          Q_LORA, KV_LORA, MLA_ROPE, MLA_HEADS * MLA_V)
_OFFS = tuple(sum(_SIZES[:i]) for i in range(len(_SIZES) + 1))

VMEM_LIMIT = 56 * 1024 * 1024

_NT = (((1,), (1,)), ((), ()))
_TN = (((0,), (0,)), ((), ()))
_F32 = jnp.float32
_BF16 = jnp.bfloat16


def _silu(v):
    return v * (0.5 + 0.5 * jnp.tanh(0.5 * v))


def _params(sem):
    return pltpu.CompilerParams(dimension_semantics=sem, vmem_limit_bytes=VMEM_LIMIT)


def _ada_kernel(c_ref, w_ref, b_ref, o_ref):
    a = _silu(c_ref[...]).astype(_BF16)
    o_ref[...] = jnp.dot(a, w_ref[...].astype(_BF16), preferred_element_type=_F32) + b_ref[...]


def _ada(c, w_ada, b_ada):
    bsz, d = c.shape
    n = w_ada.shape[1]
    tn = 1024
    return pl.pallas_call(
        _ada_kernel,
        grid=(n // tn,),
        in_specs=[pl.BlockSpec((bsz, d), lambda j: (0, 0)),
                  pl.BlockSpec((d, tn), lambda j: (0, j)),
                  pl.BlockSpec((1, tn), lambda j: (0, j))],
        out_specs=pl.BlockSpec((bsz, tn), lambda j: (0, j)),
        out_shape=jax.ShapeDtypeStruct((bsz, n), _F32),
        compiler_params=_params(("arbitrary",)),
        name="ada_mod",
    )(c, w_ada, b_ada.reshape(1, n))


def _inproj_kernel(x_ref, pos_ref, mod_ref, nw_ref, w_ref, inv64_ref, inv32_ref,
                   qnw_ref, kvnw_ref, krw_ref,
                   rq_ref, rk_ref, rv_ref, rg_ref, cq_ref, ckv_ref, kr_ref, mg_ref,
                   cos_ref, sin_ref):
    x = x_ref[0]
    shift = mod_ref[0, 0:1, :]
    scale = mod_ref[0, 1:2, :]
    y = x * lax.rsqrt(jnp.mean(x * x, axis=-1, keepdims=True) + EPS) * nw_ref[...]
    h = (y * (1.0 + scale) + shift).astype(_BF16)

    def proj(i, j=None):
        return lax.dot_general(w_ref[_OFFS[i]:_OFFS[i + 1 if j is None else j], :], h, _NT,
                               preferred_element_type=_F32)

    pos = pos_ref[0].astype(_F32)
    ang64 = inv64_ref[...] * pos
    cos64, sin64 = jnp.cos(ang64), jnp.sin(ang64)
    ang32 = inv32_ref[...] * pos
    cos32, sin32 = jnp.cos(ang32), jnp.sin(ang32)
    cos_ref[0] = cos32
    sin_ref[0] = sin32

    rg_ref[0] = _silu(proj(3)).astype(_BF16)
    mg_ref[0] = _silu(proj(7)).astype(_BF16)

    half = RET_QK // 2
    qk = proj(0, 2)
    for src, dst, mul in ((0, rq_ref, 1.0), (1, rk_ref, RET_QK ** -0.5)):
        p = qk[_OFFS[src]:_OFFS[src + 1]]
        for hd in range(RET_HEADS):
            x1 = p[hd * RET_QK: hd * RET_QK + half]
            x2 = p[hd * RET_QK + half: (hd + 1) * RET_QK]
            dst[0, hd * RET_QK: hd * RET_QK + half, :] = ((x1 * cos64 - x2 * sin64) * mul).astype(_BF16)
            dst[0, hd * RET_QK + half: (hd + 1) * RET_QK, :] = ((x2 * cos64 + x1 * sin64) * mul).astype(_BF16)

    lat = proj(4, 7)
    cq = lat[:Q_LORA]
    cq_ref[0] = (cq * lax.rsqrt(jnp.mean(cq * cq, axis=0, keepdims=True) + EPS) * qnw_ref[...]).astype(_BF16)
    ckv = lat[Q_LORA:Q_LORA + KV_LORA]
    ckv_ref[0] = (ckv * lax.rsqrt(jnp.mean(ckv * ckv, axis=0, keepdims=True) + EPS) * kvnw_ref[...]).astype(_BF16)

    kr = lat[Q_LORA + KV_LORA:]
    kr = kr * lax.rsqrt(jnp.mean(kr * kr, axis=0, keepdims=True) + EPS) * krw_ref[...]
    k1, k2 = kr[:MLA_ROPE // 2], kr[MLA_ROPE // 2:]
    kr_ref[0, :MLA_ROPE // 2, :] = (k1 * cos32 - k2 * sin32).astype(_BF16)
    kr_ref[0, MLA_ROPE // 2:, :] = (k2 * cos32 + k1 * sin32).astype(_BF16)

    rv_ref[0] = proj(2).astype(_BF16)


def _inproj(x, pos3, mod3, norm_w, w_in_t, inv64, inv32, qnw, kvnw, krw, tm):
    bsz, s, d = x.shape
    grid = (bsz, s // tm)
    const = lambda b, i: (0, 0)
    tok = lambda b, i: (b, 0, i)
    sizes = (_SIZES[0], _SIZES[1], _SIZES[2], _SIZES[3], _SIZES[4], _SIZES[5], _SIZES[6], _SIZES[7])
    out_shape = [jax.ShapeDtypeStruct((bsz, n, s), _BF16) for n in sizes]
    out_shape += [jax.ShapeDtypeStruct((bsz, MLA_ROPE // 2, s), _F32)] * 2
    out_specs = [pl.BlockSpec((1, n, tm), tok) for n in sizes]
    out_specs += [pl.BlockSpec((1, MLA_ROPE // 2, tm), tok)] * 2
    return pl.pallas_call(
        _inproj_kernel,
        grid=grid,
        in_specs=[pl.BlockSpec((1, tm, d), lambda b, i: (b, i, 0)),
                  pl.BlockSpec((1, 1, tm), tok),
                  pl.BlockSpec((1, 3, d), lambda b, i: (b, 0, 0)),
                  pl.BlockSpec((1, d), const),
                  pl.BlockSpec(w_in_t.shape, const),
                  pl.BlockSpec(inv64.shape, const),
                  pl.BlockSpec(inv32.shape, const),
                  pl.BlockSpec(qnw.shape, const),
                  pl.BlockSpec(kvnw.shape, const),
                  pl.BlockSpec(krw.shape, const)],
        out_specs=out_specs,
        out_shape=out_shape,
        compiler_params=_params(("parallel", "parallel")),
        name="in_proj",
    )(x, pos3, mod3, norm_w.reshape(1, d), w_in_t, inv64, inv32, qnw, kvnw, krw)


def _log_sigmoid(v):
    return jnp.minimum(v, 0.0) - jnp.log(1.0 + jnp.exp(-jnp.abs(v)))


def _retention_kernel(lf_ref, lb_ref, q_ref, k_ref, v_ref, g_ref, gnw_ref, o_ref,
                      kv_ref, st_ref, *, chunk, n_chunks):
    hd = pl.program_id(1)
    c = chunk
    lgf = _log_sigmoid(jnp.full((1, 1), lf_ref[hd], _F32))
    lgb = _log_sigmoid(jnp.full((1, 1), lb_ref[hd], _F32))

    pos_l = lax.broadcasted_iota(jnp.int32, (1, c), 1).astype(_F32)
    kdec_f = jnp.exp(lgf * (c - 1.0 - pos_l))
    kdec_b = jnp.exp(lgb * pos_l)
    qdec_f = jnp.exp(lgf * (pos_l + 1.0))
    qdec_b = jnp.exp(lgb * (c - pos_l))
    jj = lax.broadcasted_iota(jnp.int32, (c, c), 0).astype(_F32)
    ii = lax.broadcasted_iota(jnp.int32, (c, c), 1).astype(_F32)
    dmat = jnp.where(ii >= jj, jnp.exp(lgf * jnp.maximum(ii - jj, 0.0)),
                     jnp.exp(lgb * jnp.maximum(jj - ii, 0.0)))
    lane = lax.broadcasted_iota(jnp.int32, (1, 2 * RET_QK), 1)
    is_f = lane < RET_QK
    sdec = jnp.where(is_f, jnp.exp(lgf * c), jnp.exp(lgb * c))

    def chunk_ds(i):
        return pl.ds(pl.multiple_of(i * c, c), c)

    def kv_body(i, carry):
        ds = chunk_ds(i)
        kc = k_ref[0, :, ds].astype(_F32)
        kfb = jnp.concatenate([kc * kdec_f, kc * kdec_b], axis=0).astype(_BF16)
        kv_ref[i] = lax.dot_general(v_ref[0, :, ds], kfb, _NT, preferred_element_type=_F32)
        return carry

    lax.fori_loop(0, n_chunks, kv_body, 0, unroll=16)

    def bwd_body(t, sb):
        i = n_chunks - 1 - t
        st_ref[i] = sb
        return sb * sdec + kv_ref[i]

    lax.fori_loop(0, n_chunks, bwd_body, jnp.zeros((RET_V, 2 * RET_QK), _F32))

    gnw = gnw_ref[0]

    def out_body(i, sf):
        ds = chunk_ds(i)
        qc = q_ref[0, :, ds]
        kc = k_ref[0, :, ds]
        st = lax.dot_general(kc, qc, _TN, preferred_element_type=_F32)
        pt = (st * dmat).astype(_BF16)
        qf = qc.astype(_F32)
        qfb = jnp.concatenate([qf * qdec_f, qf * qdec_b], axis=0).astype(_BF16)
        state = jnp.where(is_f, sf, st_ref[i]).astype(_BF16)
        o = (jnp.dot(v_ref[0, :, ds], pt, preferred_element_type=_F32)
             + jnp.dot(state, qfb, preferred_element_type=_F32))
        mu = jnp.mean(o, axis=0, keepdims=True)
        dlt = o - mu
        var = jnp.mean(dlt * dlt, axis=0, keepdims=True)
        on = dlt * lax.rsqrt(var + EPS) * gnw
        o_ref[0, :, ds] = (on * g_ref[0, :, ds].astype(_F32)).astype(_BF16)
        return sf * sdec + kv_ref[i]

    lax.fori_loop(0, n_chunks, out_body, jnp.zeros((RET_V, 2 * RET_QK), _F32), unroll=16)


def _retention(lf, lb, rq_t, rk_t, rv_t, rg_t, gnw3, chunk):
    bsz, _, s = rq_t.shape
    n_chunks = s // chunk
    smem = pl.BlockSpec(memory_space=pltpu.SMEM)
    return pl.pallas_call(
        functools.partial(_retention_kernel, chunk=chunk, n_chunks=n_chunks),
        grid=(bsz, RET_HEADS),
        in_specs=[smem, smem,
                  pl.BlockSpec((1, RET_QK, s), lambda b, h: (b, h, 0)),
                  pl.BlockSpec((1, RET_QK, s), lambda b, h: (b, h, 0)),
                  pl.BlockSpec((1, RET_V, s), lambda b, h: (b, h, 0)),
                  pl.BlockSpec((1, RET_V, s), lambda b, h: (b, h, 0)),
                  pl.BlockSpec((1, RET_V, 1), lambda b, h: (h, 0, 0))],
        out_specs=pl.BlockSpec((1, RET_V, s), lambda b, h: (b, h, 0)),
        out_shape=jax.ShapeDtypeStruct((bsz, RET_HEADS * RET_V, s), _BF16),
        scratch_shapes=[pltpu.VMEM((n_chunks, RET_V, 2 * RET_QK), _F32),
                        pltpu.VMEM((n_chunks, RET_V, 2 * RET_QK), _F32)],
        compiler_params=_params(("parallel", "parallel")),
        name="retention",
    )(lf, lb, rq_t, rk_t, rv_t, rg_t, gnw3)


def _max_sq_norm(v):
    return jnp.max(jnp.sum(v * v, axis=0, keepdims=True), axis=1, keepdims=True)


def _mla_up_kernel(cq_ref, ckv_ref, kr_ref, cos_ref, sin_ref, wq_ref, wkv_ref,
                   qnw_ref, qrw_ref, knw_ref, qt_ref, k_ref, vt_ref, qn2_ref, kn2_ref, *, q_mul):
    tm = cq_ref.shape[2]
    q_all = jnp.dot(wq_ref[...], cq_ref[0], preferred_element_type=_F32)
    kv_all = jnp.dot(wkv_ref[...], ckv_ref[0], preferred_element_type=_F32)
    cos, sin = cos_ref[0], sin_ref[0]
    kr = kr_ref[0].astype(_F32)
    zpad = jnp.zeros((HEAD_PAD - MLA_NOPE - MLA_ROPE, tm), _F32)
    hq = MLA_NOPE + MLA_ROPE
    hkv = MLA_NOPE + MLA_V
    r2 = MLA_ROPE // 2
    for hd in range(MLA_HEADS):
        qn = q_all[hd * hq: hd * hq + MLA_NOPE]
        qr = q_all[hd * hq + MLA_NOPE: (hd + 1) * hq]
        qn = qn * lax.rsqrt(jnp.mean(qn * qn, axis=0, keepdims=True) + EPS) * qnw_ref[...]
        qr = qr * lax.rsqrt(jnp.mean(qr * qr, axis=0, keepdims=True) + EPS) * qrw_ref[...]
        q1, q2 = qr[:r2], qr[r2:]
        qh = jnp.concatenate([qn, q1 * cos - q2 * sin, q2 * cos + q1 * sin, zpad], axis=0)
        qh = qh * q_mul
        qt_ref[0, hd] = qh.astype(_BF16)
        qn2_ref[0, 0, hd:hd + 1, :] = jnp.broadcast_to(_max_sq_norm(qh), (1, HEAD_PAD))
        kn = kv_all[hd * hkv: hd * hkv + MLA_NOPE]
        kn = kn * lax.rsqrt(jnp.mean(kn * kn, axis=0, keepdims=True) + EPS) * knw_ref[...]
        kh = jnp.concatenate([kn, kr, zpad], axis=0)
        k_ref[0, hd] = kh.T.astype(_BF16)
        kn2_ref[0, 0, hd:hd + 1, :] = jnp.broadcast_to(_max_sq_norm(kh), (1, HEAD_PAD))
        vt_ref[0, hd] = kv_all[hd * hkv + MLA_NOPE: (hd + 1) * hkv].astype(_BF16)


def _mla_up(cq_t, ckv_t, kr_t, cos_t, sin_t, wq_t, wkv_t, qnw, qrw, knw, tm):
    bsz, _, s = cq_t.shape
    tok = lambda b, i: (b, 0, i)
    const = lambda b, i: (0, 0)
    q_mul = (MLA_NOPE + MLA_ROPE) ** -0.5 * LOG2E
    return pl.pallas_call(
        functools.partial(_mla_up_kernel, q_mul=q_mul),
        grid=(bsz, s // tm),
        in_specs=[pl.BlockSpec((1, Q_LORA, tm), tok),
                  pl.BlockSpec((1, KV_LORA, tm), tok),
                  pl.BlockSpec((1, MLA_ROPE, tm), tok),
                  pl.BlockSpec((1, MLA_ROPE // 2, tm), tok),
                  pl.BlockSpec((1, MLA_ROPE // 2, tm), tok),
                  pl.BlockSpec(wq_t.shape, const),
                  pl.BlockSpec(wkv_t.shape, const),
                  pl.BlockSpec(qnw.shape, const),
                  pl.BlockSpec(qrw.shape, const),
                  pl.BlockSpec(knw.shape, const)],
        out_specs=[pl.BlockSpec((1, MLA_HEADS, HEAD_PAD, tm), lambda b, i: (b, 0, 0, i)),
                   pl.BlockSpec((1, MLA_HEADS, tm, HEAD_PAD), lambda b, i: (b, 0, i, 0)),
                   pl.BlockSpec((1, MLA_HEADS, MLA_V, tm), lambda b, i: (b, 0, 0, i)),
                   pl.BlockSpec((1, 1, MLA_HEADS, HEAD_PAD), lambda b, i: (b, i, 0, 0)),
                   pl.BlockSpec((1, 1, MLA_HEADS, HEAD_PAD), lambda b, i: (b, i, 0, 0))],
        out_shape=[jax.ShapeDtypeStruct((bsz, MLA_HEADS, HEAD_PAD, s), _BF16),
                   jax.ShapeDtypeStruct((bsz, MLA_HEADS, s, HEAD_PAD), _BF16),
                   jax.ShapeDtypeStruct((bsz, MLA_HEADS, MLA_V, s), _BF16),
                   jax.ShapeDtypeStruct((bsz, s // tm, MLA_HEADS, HEAD_PAD), _F32),
                   jax.ShapeDtypeStruct((bsz, s // tm, MLA_HEADS, HEAD_PAD), _F32)],
        compiler_params=_params(("parallel", "parallel")),
        name="mla_up",
    )(cq_t, ckv_t, kr_t, cos_t, sin_t, wq_t, wkv_t, qnw, qrw, knw)


def _attn_kernel(fast_ref, qt_ref, k_ref, vt_ref, o_ref, *bufs, tq, tk, tqf, tkf, n_q, n_kv):
    nb = len(bufs)
    fast = fast_ref[pl.program_id(0) * pl.num_programs(1) + pl.program_id(1)]

    def q_ds(qi):
        return pl.ds(pl.multiple_of(qi * tq, tq), tq)

    def kv_ds(j):
        return pl.ds(pl.multiple_of(j * tk, tk), tk)

    def write(qi, l, acc):
        o_ref[0, :, q_ds(qi)] = (acc * (1.0 / l)).astype(_BF16)

    @pl.when(fast != 0)
    def _fast_path():
        def body(qi, carry):
            qds = pl.ds(pl.multiple_of(qi * tqf, tqf), tqf)
            qt = qt_ref[0, 0, :, qds]
            l = jnp.zeros((1, tqf), _F32)
            acc = jnp.zeros((MLA_V, tqf), _F32)
            for j in range(n_kv * tk // tkf):
                ds = pl.ds(j * tkf, tkf)
                st = jnp.dot(k_ref[0, 0, ds, :], qt, preferred_element_type=_F32)
                p = jnp.exp2(st)
                l = l + jnp.sum(p, axis=0, keepdims=True)
                acc = acc + jnp.dot(vt_ref[0, 0, :, ds], p.astype(_BF16),
                                    preferred_element_type=_F32)
            o_ref[0, :, qds] = (acc * (1.0 / l)).astype(_BF16)
            return carry

        lax.fori_loop(0, n_q * tq // tqf, body, 0)

    @pl.when(fast == 0)
    def _safe_path():
        def scores(qi, j, m, s_ref):
            st = jnp.dot(k_ref[0, 0, kv_ds(j), :], qt_ref[0, 0, :, q_ds(qi)],
                         preferred_element_type=_F32)
            s_ref[...] = st
            return jnp.maximum(m, jnp.max(st, axis=0, keepdims=True))

        def accumulate(j, m_prev, m, l, acc, s_ref):
            alpha = jnp.exp2(m_prev - m)
            p = jnp.exp2(s_ref[...] - m)
            l = alpha * l + jnp.sum(p, axis=0, keepdims=True)
            acc = alpha * acc + jnp.dot(vt_ref[0, 0, :, kv_ds(j)], p.astype(_BF16),
                                        preferred_element_type=_F32)
            return l, acc

        neg_inf = jnp.full((1, tq), -jnp.inf, _F32)

        def body(qi, carry):
            ms = [neg_inf] + list(carry)
            nxt = []
            qn = jnp.minimum(qi + 1, n_q - 1)
            l = jnp.zeros((1, tq), _F32)
            acc = jnp.zeros((MLA_V, tq), _F32)
            for j in range(n_kv):
                t = j + ATTN_AHEAD
                if t < n_kv:
                    ms.append(scores(qi, t, ms[-1], bufs[t % nb]))
                else:
                    nxt.append(scores(qn, t - n_kv, nxt[-1] if nxt else neg_inf, bufs[t % nb]))
                l, acc = accumulate(j, ms[j], ms[j + 1], l, acc, bufs[j % nb])
            write(qi, l, acc)
            return tuple(nxt)

        first = []
        for j in range(ATTN_AHEAD):
            first.append(scores(0, j, first[-1] if first else neg_inf, bufs[j]))
        lax.fori_loop(0, n_q, body, tuple(first))


ATTN_AHEAD = 2
ATTN_BUFS = 4
SCORE_BOUND_LOG2 = 60.0
ATTN_FAST_TK = 4096
ATTN_FAST_TQ = 1024


def _attention(fast, qt, k, vt, tq, tk):
    bsz, nh, _, s = qt.shape
    n_kv = s // tk
    assert s % tq == 0 and s % tk == 0 and n_kv % ATTN_BUFS == 0 and ATTN_AHEAD < ATTN_BUFS
    return pl.pallas_call(
        functools.partial(_attn_kernel, tq=tq, tk=tk, tqf=min(ATTN_FAST_TQ, s), tkf=min(ATTN_FAST_TK, s),
                          n_q=s // tq, n_kv=s // tk),
        grid=(bsz, nh),
        in_specs=[pl.BlockSpec(memory_space=pltpu.SMEM),
                  pl.BlockSpec((1, 1, HEAD_PAD, s), lambda b, h: (b, h, 0, 0)),
                  pl.BlockSpec((1, 1, s, HEAD_PAD), lambda b, h: (b, h, 0, 0)),
                  pl.BlockSpec((1, 1, MLA_V, s), lambda b, h: (b, h, 0, 0))],
        out_specs=pl.BlockSpec((1, MLA_V, s), lambda b, h: (b, h, 0)),
        out_shape=jax.ShapeDtypeStruct((bsz, nh * MLA_V, s), _BF16),
        scratch_shapes=[pltpu.VMEM((tk, tq), _F32)] * ATTN_BUFS,
        compiler_params=_params(("parallel", "parallel")),
        name="mla_attention",
    )(fast, qt, k, vt)


def _out_kernel(x_ref, ret_ref, att_ref, mg_ref, mod_ref, onw_ref, w_ref, o_ref):
    att = att_ref[0].astype(_F32)
    mla = att * lax.rsqrt(jnp.mean(att * att, axis=0, keepdims=True) + EPS) * onw_ref[...]
    mla = (mla * mg_ref[0].astype(_F32)).astype(_BF16)
    cat = jnp.concatenate([ret_ref[0], mla], axis=0)
    yt = jnp.dot(w_ref[...], cat, preferred_element_type=_F32)
    gate = mod_ref[0, 2:3, :]
    o_ref[0] = x_ref[0] + gate * yt.T


def _out_proj(x, ret_t, att_t, mg_t, mod3, onw, w_out_t, tm):
    bsz, s, d = x.shape
    tok = lambda b, i: (b, 0, i)
    const = lambda b, i: (0, 0)
    return pl.pallas_call(
        _out_kernel,
        grid=(bsz, s // tm),
        in_specs=[pl.BlockSpec((1, tm, d), lambda b, i: (b, i, 0)),
                  pl.BlockSpec((1, ret_t.shape[1], tm), tok),
                  pl.BlockSpec((1, att_t.shape[1], tm), tok),
                  pl.BlockSpec((1, mg_t.shape[1], tm), tok),
                  pl.BlockSpec((1, 3, d), lambda b, i: (b, 0, 0)),
                  pl.BlockSpec(onw.shape, const),
                  pl.BlockSpec(w_out_t.shape, const)],
        out_specs=pl.BlockSpec((1, tm, d), lambda b, i: (b, i, 0)),
        out_shape=jax.ShapeDtypeStruct((bsz, s, d), x.dtype),
        compiler_params=_params(("parallel", "parallel")),
        name="out_proj",
    )(x, ret_t, att_t, mg_t, mod3, onw, w_out_t)


def _score_bound_ok(qn2, kn2):
    bound2 = jnp.max(qn2, axis=(1, 3)) * jnp.max(kn2, axis=(1, 3))
    return (bound2 <= SCORE_BOUND_LOG2 ** 2).astype(jnp.int32).reshape(-1)


def _col(v):
    return v.astype(_F32).reshape(-1, 1)


def _block_sizes(s):
    tm = min(512, s)
    tq = min(512, s)
    tk = min(1024, s // ATTN_BUFS)
    chunk = min(256, s)
    return tm, tq, tk, chunk


def kernel(x, c, positions, norm_w, w_ada, b_ada, w_in, ret_decay_logit_fwd, ret_decay_logit_bwd, ret_gn_w, q_norm_w, w_uq, kv_norm_w, w_ukv, qn_nope_w, qn_rope_w, kn_nope_w, kn_rope_w, mla_out_norm_w, w_out):
    bsz, s, d = x.shape
    tm, tq, tk, chunk = _block_sizes(s)

    w_in_t = w_in.T.astype(_BF16)
    wq_t = w_uq.T.astype(_BF16)
    wkv_t = w_ukv.T.astype(_BF16)
    w_out_t = w_out.T.astype(_BF16)
    inv64 = (ROPE_BASE ** (-jnp.arange(0, RET_QK, 2, dtype=_F32) / RET_QK)).reshape(-1, 1)
    inv32 = (ROPE_BASE ** (-jnp.arange(0, MLA_ROPE, 2, dtype=_F32) / MLA_ROPE)).reshape(-1, 1)

    mod3 = _ada(c, w_ada, b_ada).reshape(bsz, 3, d)
    pos3 = positions.reshape(bsz, 1, s)

    (rq_t, rk_t, rv_t, rg_t, cq_t, ckv_t, kr_t, mg_t, cos_t, sin_t) = _inproj(
        x, pos3, mod3, norm_w, w_in_t, inv64, inv32,
        _col(q_norm_w), _col(kv_norm_w), _col(kn_rope_w), tm)

    ret_t = _retention(ret_decay_logit_fwd.astype(_F32), ret_decay_logit_bwd.astype(_F32),
                       rq_t, rk_t, rv_t, rg_t,
                       ret_gn_w.astype(_F32).reshape(RET_HEADS, RET_V, 1), chunk)

    qt, k, vt, qn2, kn2 = _mla_up(cq_t, ckv_t, kr_t, cos_t, sin_t, wq_t, wkv_t,
                                  _col(qn_nope_w), _col(qn_rope_w), _col(kn_nope_w), tm)
    att_t = _attention(_score_bound_ok(qn2, kn2), qt, k, vt, tq, tk)

    return _out_proj(x, ret_t, att_t, mg_t, mod3, _col(mla_out_norm_w), w_out_t, tm)
```

```python
import functools
import math

import jax
import jax.numpy as jnp
from jax import lax
from jax.experimental import pallas as pl
from jax.experimental.pallas import tpu as pltpu

RET_HEADS = 8
RET_QK = 64
RET_V = 128
MLA_HEADS = 8
MLA_NOPE = 64
MLA_ROPE = 32
MLA_V = 128
Q_LORA = 384
KV_LORA = 256
ROPE_BASE = 10000.0
EPS = 1e-6
HEAD_PAD = 128
LOG2E = 1.4426950408889634

_SIZES = (RET_HEADS * RET_QK, RET_HEADS * RET_QK, RET_HEADS * RET_V, RET_HEADS * RET_V,
          Q_LORA, KV_LORA, MLA_ROPE, MLA_HEADS * MLA_V)
_OFFS = tuple(sum(_SIZES[:i]) for i in range(len(_SIZES) + 1))

VMEM_LIMIT = 56 * 1024 * 1024

_NT = (((1,), (1,)), ((), ()))
_TN = (((0,), (0,)), ((), ()))
_F32 = jnp.float32
_BF16 = jnp.bfloat16


def _silu(v):
    return v * (0.5 + 0.5 * jnp.tanh(0.5 * v))


def _params(sem):
    return pltpu.CompilerParams(dimension_semantics=sem, vmem_limit_bytes=VMEM_LIMIT)


def _ada_kernel(c_ref, w_ref, b_ref, o_ref):
    a = _silu(c_ref[...]).astype(_BF16)
    o_ref[...] = jnp.dot(a, w_ref[...].astype(_BF16), preferred_element_type=_F32) + b_ref[...]


def _ada(c, w_ada, b_ada):
    bsz, d = c.shape
    n = w_ada.shape[1]
    tn = 1024
    return pl.pallas_call(
        _ada_kernel,
        grid=(n // tn,),
        in_specs=[pl.BlockSpec((bsz, d), lambda j: (0, 0)),
                  pl.BlockSpec((d, tn), lambda j: (0, j)),
                  pl.BlockSpec((1, tn), lambda j: (0, j))],
        out_specs=pl.BlockSpec((bsz, tn), lambda j: (0, j)),
        out_shape=jax.ShapeDtypeStruct((bsz, n), _F32),
        compiler_params=_params(("arbitrary",)),
        name="ada_mod",
    )(c, w_ada, b_ada.reshape(1, n))


def _inproj_kernel(x_ref, pos_ref, mod_ref, nw_ref, w_ref, inv64_ref, inv32_ref,
                   qnw_ref, kvnw_ref, krw_ref, wq_ref, wkv_ref, qnn_ref, qrn_ref, knn_ref,
                   rq_ref, rk_ref, rv_ref, rg_ref, mg_ref,
                   qt_ref, k_ref, vt_ref, qn2_ref, kn2_ref, *, q_mul):
    tm = x_ref.shape[1]
    x = x_ref[0]
    shift = mod_ref[0, 0:1, :]
    scale = mod_ref[0, 1:2, :]
    y = x * lax.rsqrt(jnp.mean(x * x, axis=-1, keepdims=True) + EPS) * nw_ref[...]
    h = (y * (1.0 + scale) + shift).astype(_BF16)

    def proj(i, j=None):
        return lax.dot_general(w_ref[_OFFS[i]:_OFFS[i + 1 if j is None else j], :], h, _NT,
                               preferred_element_type=_F32)

    def rms(v, w_col):
        return v * lax.rsqrt(jnp.mean(v * v, axis=0, keepdims=True) + EPS) * w_col

    def rot(v, cos, sin):
        n = v.shape[0] // 2
        return v[:n] * cos - v[n:] * sin, v[n:] * cos + v[:n] * sin

    pos = pos_ref[0].astype(_F32)
    ang64 = inv64_ref[...] * pos
    cos64, sin64 = jnp.cos(ang64), jnp.sin(ang64)
    ang32 = inv32_ref[...] * pos
    cos32, sin32 = jnp.cos(ang32), jnp.sin(ang32)

    lat = proj(4, 7)
    cq = rms(lat[:Q_LORA], qnw_ref[...]).astype(_BF16)
    ckv = rms(lat[Q_LORA:Q_LORA + KV_LORA], kvnw_ref[...]).astype(_BF16)
    kr = jnp.concatenate(rot(rms(lat[Q_LORA + KV_LORA:], krw_ref[...]), cos32, sin32), axis=0)
    q_all = jnp.dot(wq_ref[...], cq, preferred_element_type=_F32)
    kv_all = jnp.dot(wkv_ref[...], ckv, preferred_element_type=_F32)
    zpad = jnp.zeros((HEAD_PAD - MLA_NOPE - MLA_ROPE, tm), _F32)
    hq = MLA_NOPE + MLA_ROPE
    hkv = MLA_NOPE + MLA_V
    for hd in range(MLA_HEADS):
        qn = rms(q_all[hd * hq: hd * hq + MLA_NOPE], qnn_ref[...])
        qr = rot(rms(q_all[hd * hq + MLA_NOPE: (hd + 1) * hq], qrn_ref[...]), cos32, sin32)
        qh = jnp.concatenate([qn, qr[0], qr[1], zpad], axis=0) * q_mul
        qt_ref[0, hd] = qh.astype(_BF16)
        qn2_ref[0, 0, hd:hd + 1, :] = jnp.broadcast_to(_max_sq_norm(qh), (1, HEAD_PAD))
        kn = rms(kv_all[hd * hkv: hd * hkv + MLA_NOPE], knn_ref[...])
        kh = jnp.concatenate([kn, kr, zpad], axis=0)
        k_ref[0, hd] = kh.T.astype(_BF16)
        kn2_ref[0, 0, hd:hd + 1, :] = jnp.broadcast_to(_max_sq_norm(kh), (1, HEAD_PAD))
        vt_ref[0, hd] = kv_all[hd * hkv + MLA_NOPE: (hd + 1) * hkv].astype(_BF16)

    rg_ref[0] = _silu(proj(3)).astype(_BF16)
    mg_ref[0] = _silu(proj(7)).astype(_BF16)
    qk = proj(0, 2)
    for src, dst, mul in ((0, rq_ref, 1.0), (1, rk_ref, RET_QK ** -0.5)):
        for hd in range(RET_HEADS):
            lo = _OFFS[src] + hd * RET_QK
            r1, r2 = rot(qk[lo: lo + RET_QK], cos64, sin64)
            dst[0, hd * RET_QK: hd * RET_QK + RET_QK // 2, :] = (r1 * mul).astype(_BF16)
            dst[0, hd * RET_QK + RET_QK // 2: (hd + 1) * RET_QK, :] = (r2 * mul).astype(_BF16)
    rv_ref[0] = proj(2).astype(_BF16)


def _max_sq_norm(v):
    return jnp.max(jnp.sum(v * v, axis=0, keepdims=True), axis=1, keepdims=True)


def _inproj(x, pos3, mod3, norm_w, w_in_t, inv64, inv32, qnw, kvnw, krw, wq_t, wkv_t, qnn, qrn, knn, tm):
    bsz, s, d = x.shape
    const = lambda b, i: (0, 0)
    tok = lambda b, i: (b, 0, i)
    head_t = lambda b, i: (b, 0, 0, i)
    sizes = (_SIZES[0], _SIZES[1], _SIZES[2], _SIZES[3], _SIZES[7])
    out_shape = [jax.ShapeDtypeStruct((bsz, n, s), _BF16) for n in sizes]
    out_specs = [pl.BlockSpec((1, n, tm), tok) for n in sizes]
    out_shape += [jax.ShapeDtypeStruct((bsz, MLA_HEADS, HEAD_PAD, s), _BF16),
                  jax.ShapeDtypeStruct((bsz, MLA_HEADS, s, HEAD_PAD), _BF16),
                  jax.ShapeDtypeStruct((bsz, MLA_HEADS, MLA_V, s), _BF16),
                  jax.ShapeDtypeStruct((bsz, s // tm, MLA_HEADS, HEAD_PAD), _F32),
                  jax.ShapeDtypeStruct((bsz, s // tm, MLA_HEADS, HEAD_PAD), _F32)]
    out_specs += [pl.BlockSpec((1, MLA_HEADS, HEAD_PAD, tm), head_t),
                  pl.BlockSpec((1, MLA_HEADS, tm, HEAD_PAD), lambda b, i: (b, 0, i, 0)),
                  pl.BlockSpec((1, MLA_HEADS, MLA_V, tm), head_t),
                  pl.BlockSpec((1, 1, MLA_HEADS, HEAD_PAD), lambda b, i: (b, i, 0, 0)),
                  pl.BlockSpec((1, 1, MLA_HEADS, HEAD_PAD), lambda b, i: (b, i, 0, 0))]
    consts = (w_in_t, inv64, inv32, qnw, kvnw, krw, wq_t, wkv_t, qnn, qrn, knn)
    q_mul = (MLA_NOPE + MLA_ROPE) ** -0.5 * LOG2E
    return pl.pallas_call(
        functools.partial(_inproj_kernel, q_mul=q_mul),
        grid=(bsz, s // tm),
        in_specs=[pl.BlockSpec((1, tm, d), lambda b, i: (b, i, 0)),
                  pl.BlockSpec((1, 1, tm), tok),
                  pl.BlockSpec((1, 3, d), lambda b, i: (b, 0, 0)),
                  pl.BlockSpec((1, d), const)] + [pl.BlockSpec(a.shape, const) for a in consts],
        out_specs=out_specs,
        out_shape=out_shape,
        compiler_params=_params(("parallel", "parallel")),
        name="in_proj",
    )(x, pos3, mod3, norm_w.reshape(1, d), *consts)


def _log_sigmoid(v):
    return jnp.minimum(v, 0.0) - jnp.log(1.0 + jnp.exp(-jnp.abs(v)))


def _retention_kernel(lf_ref, lb_ref, q_ref, k_ref, v_ref, g_ref, gnw_ref, o_ref,
                      kv_ref, st_ref, *, chunk, n_chunks):
    hd = pl.program_id(1)
    c = chunk
    lgf = _log_sigmoid(jnp.full((1, 1), lf_ref[hd], _F32))
    lgb = _log_sigmoid(jnp.full((1, 1), lb_ref[hd], _F32))

    pos_l = lax.broadcasted_iota(jnp.int32, (1, c), 1).astype(_F32)
    kdec_f = jnp.exp(lgf * (c - 1.0 - pos_l))
    kdec_b = jnp.exp(lgb * pos_l)
    qdec_f = jnp.exp(lgf * (pos_l + 1.0))
    qdec_b = jnp.exp(lgb * (c - pos_l))
    jj = lax.broadcasted_iota(jnp.int32, (c, c), 0).astype(_F32)
    ii = lax.broadcasted_iota(jnp.int32, (c, c), 1).astype(_F32)
    dmat = jnp.where(ii >= jj, jnp.exp(lgf * jnp.maximum(ii - jj, 0.0)),
                     jnp.exp(lgb * jnp.maximum(jj - ii, 0.0)))
    lane = lax.broadcasted_iota(jnp.int32, (1, 2 * RET_QK), 1)
    is_f = lane < RET_QK
    sdec = jnp.where(is_f, jnp.exp(lgf * c), jnp.exp(lgb * c))

    def chunk_ds(i):
        return pl.ds(pl.multiple_of(i * c, c), c)

    def kv_body(i, carry):
        ds = chunk_ds(i)
        kc = k_ref[0, :, ds].astype(_F32)
        kfb = jnp.concatenate([kc * kdec_f, kc * kdec_b], axis=0).astype(_BF16)
        kv_ref[i] = lax.dot_general(v_ref[0, :, ds], kfb, _NT, preferred_element_type=_F32)
        return carry

    lax.fori_loop(0, n_chunks, kv_body, 0, unroll=16)

    def bwd_body(t, sb):
        i = n_chunks - 1 - t
        st_ref[i] = sb
        return sb * sdec + kv_ref[i]

    lax.fori_loop(0, n_chunks, bwd_body, jnp.zeros((RET_V, 2 * RET_QK), _F32))

    gnw = gnw_ref[0]

    def out_body(i, sf):
        ds = chunk_ds(i)
        qc = q_ref[0, :, ds]
        kc = k_ref[0, :, ds]
        st = lax.dot_general(kc, qc, _TN, preferred_element_type=_F32)
        pt = (st * dmat).astype(_BF16)
        qf = qc.astype(_F32)
        qfb = jnp.concatenate([qf * qdec_f, qf * qdec_b], axis=0).astype(_BF16)
        state = jnp.where(is_f, sf, st_ref[i]).astype(_BF16)
        o = (jnp.dot(v_ref[0, :, ds], pt, preferred_element_type=_F32)
             + jnp.dot(state, qfb, preferred_element_type=_F32))
        mu = jnp.mean(o, axis=0, keepdims=True)
        dlt = o - mu
        var = jnp.mean(dlt * dlt, axis=0, keepdims=True)
        on = dlt * lax.rsqrt(var + EPS) * gnw
        o_ref[0, :, ds] = (on * g_ref[0, :, ds].astype(_F32)).astype(_BF16)
        return sf * sdec + kv_ref[i]

    lax.fori_loop(0, n_chunks, out_body, jnp.zeros((RET_V, 2 * RET_QK), _F32), unroll=16)


def _retention(lf, lb, rq_t, rk_t, rv_t, rg_t, gnw3, chunk):
    bsz, _, s = rq_t.shape
    n_chunks = s // chunk
    smem = pl.BlockSpec(memory_space=pltpu.SMEM)
    return pl.pallas_call(
        functools.partial(_retention_kernel, chunk=chunk, n_chunks=n_chunks),
        grid=(bsz, RET_HEADS),
        in_specs=[smem, smem,
                  pl.BlockSpec((1, RET_QK, s), lambda b, h: (b, h, 0)),
                  pl.BlockSpec((1, RET_QK, s), lambda b, h: (b, h, 0)),
                  pl.BlockSpec((1, RET_V, s), lambda b, h: (b, h, 0)),
                  pl.BlockSpec((1, RET_V, s), lambda b, h: (b, h, 0)),
                  pl.BlockSpec((1, RET_V, 1), lambda b, h: (h, 0, 0))],
        out_specs=pl.BlockSpec((1, RET_V, s), lambda b, h: (b, h, 0)),
        out_shape=jax.ShapeDtypeStruct((bsz, RET_HEADS * RET_V, s), _BF16),
        scratch_shapes=[pltpu.VMEM((n_chunks, RET_V, 2 * RET_QK), _F32),
                        pltpu.VMEM((n_chunks, RET_V, 2 * RET_QK), _F32)],
        compiler_params=_params(("parallel", "parallel")),
        name="retention",
    )(lf, lb, rq_t, rk_t, rv_t, rg_t, gnw3)


def _attn_kernel(fast_ref, qt_ref, k_ref, vt_ref, o_ref, *bufs, tq, tk, tqf, tkf, n_q, n_kv):
    nb = len(bufs)
    fast = fast_ref[pl.program_id(0) * pl.num_programs(1) + pl.program_id(1)]

    def q_ds(qi):
        return pl.ds(pl.multiple_of(qi * tq, tq), tq)

    def kv_ds(j):
        return pl.ds(pl.multiple_of(j * tk, tk), tk)

    def write(qi, l, acc):
        o_ref[0, :, q_ds(qi)] = (acc * (1.0 / l)).astype(_BF16)

    @pl.when(fast != 0)
    def _fast_path():
        def body(qi, carry):
            qds = pl.ds(pl.multiple_of(qi * tqf, tqf), tqf)
            qt = qt_ref[0, 0, :, qds]
            l = jnp.zeros((1, tqf), _F32)
            acc = jnp.zeros((MLA_V, tqf), _F32)
            for j in range(n_kv * tk // tkf):
                ds = pl.ds(j * tkf, tkf)
                st = jnp.dot(k_ref[0, 0, ds, :], qt, preferred_element_type=_F32)
                p = jnp.exp2(st)
                l = l + jnp.sum(p, axis=0, keepdims=True)
                acc = acc + jnp.dot(vt_ref[0, 0, :, ds], p.astype(_BF16),
                                    preferred_element_type=_F32)
            o_ref[0, :, qds] = (acc * (1.0 / l)).astype(_BF16)
            return carry

        lax.fori_loop(0, n_q * tq // tqf, body, 0)

    @pl.when(fast == 0)
    def _safe_path():
        def scores(qi, j, m, s_ref):
            st = jnp.dot(k_ref[0, 0, kv_ds(j), :], qt_ref[0, 0, :, q_ds(qi)],
                         preferred_element_type=_F32)
            s_ref[...] = st
            return jnp.maximum(m, jnp.max(st, axis=0, keepdims=True))

        def accumulate(j, m_prev, m, l, acc, s_ref):
            alpha = jnp.exp2(m_prev - m)
            p = jnp.exp2(s_ref[...] - m)
            l = alpha * l + jnp.sum(p, axis=0, keepdims=True)
            acc = alpha * acc + jnp.dot(vt_ref[0, 0, :, kv_ds(j)], p.astype(_BF16),
                                        preferred_element_type=_F32)
            return l, acc

        neg_inf = jnp.full((1, tq), -jnp.inf, _F32)

        def body(qi, carry):
            ms = [neg_inf] + list(carry)
            nxt = []
            qn = jnp.minimum(qi + 1, n_q - 1)
            l = jnp.zeros((1, tq), _F32)
            acc = jnp.zeros((MLA_V, tq), _F32)
            for j in range(n_kv):
                t = j + ATTN_AHEAD
                if t < n_kv:
                    ms.append(scores(qi, t, ms[-1], bufs[t % nb]))
                else:
                    nxt.append(scores(qn, t - n_kv, nxt[-1] if nxt else neg_inf, bufs[t % nb]))
                l, acc = accumulate(j, ms[j], ms[j + 1], l, acc, bufs[j % nb])
            write(qi, l, acc)
            return tuple(nxt)

        first = []
        for j in range(ATTN_AHEAD):
            first.append(scores(0, j, first[-1] if first else neg_inf, bufs[j]))
        lax.fori_loop(0, n_q, body, tuple(first))


ATTN_AHEAD = 2
ATTN_BUFS = 4
SCORE_BOUND_LOG2 = 60.0
ATTN_FAST_TK = 4096
ATTN_FAST_TQ = 1024


def _attention(fast, qt, k, vt, tq, tk):
    bsz, nh, _, s = qt.shape
    n_kv = s // tk
    assert s % tq == 0 and s % tk == 0 and n_kv % ATTN_BUFS == 0 and ATTN_AHEAD < ATTN_BUFS
    return pl.pallas_call(
        functools.partial(_attn_kernel, tq=tq, tk=tk, tqf=min(ATTN_FAST_TQ, s), tkf=min(ATTN_FAST_TK, s),
                          n_q=s // tq, n_kv=s // tk),
        grid=(bsz, nh),
        in_specs=[pl.BlockSpec(memory_space=pltpu.SMEM),
                  pl.BlockSpec((1, 1, HEAD_PAD, s), lambda b, h: (b, h, 0, 0)),
                  pl.BlockSpec((1, 1, s, HEAD_PAD), lambda b, h: (b, h, 0, 0)),
                  pl.BlockSpec((1, 1, MLA_V, s), lambda b, h: (b, h, 0, 0))],
        out_specs=pl.BlockSpec((1, MLA_V, s), lambda b, h: (b, h, 0)),
        out_shape=jax.ShapeDtypeStruct((bsz, nh * MLA_V, s), _BF16),
        scratch_shapes=[pltpu.VMEM((tk, tq), _F32)] * ATTN_BUFS,
        compiler_params=_params(("parallel", "parallel")),
        name="mla_attention",
    )(fast, qt, k, vt)


def _out_kernel(x_ref, ret_ref, att_ref, mg_ref, mod_ref, onw_ref, w_ref, o_ref):
    att = att_ref[0].astype(_F32)
    mla = att * lax.rsqrt(jnp.mean(att * att, axis=0, keepdims=True) + EPS) * onw_ref[...]
    mla = (mla * mg_ref[0].astype(_F32)).astype(_BF16)
    cat = jnp.concatenate([ret_ref[0], mla], axis=0)
    yt = jnp.dot(w_ref[...], cat, preferred_element_type=_F32)
    gate = mod_ref[0, 2:3, :]
    o_ref[0] = x_ref[0] + gate * yt.T


def _out_proj(x, ret_t, att_t, mg_t, mod3, onw, w_out_t, tm):
    bsz, s, d = x.shape
    tok = lambda b, i: (b, 0, i)
    const = lambda b, i: (0, 0)
    return pl.pallas_call(
        _out_kernel,
        grid=(bsz, s // tm),
        in_specs=[pl.BlockSpec((1, tm, d), lambda b, i: (b, i, 0)),
                  pl.BlockSpec((1, ret_t.shape[1], tm), tok),
                  pl.BlockSpec((1, att_t.shape[1], tm), tok),
                  pl.BlockSpec((1, mg_t.shape[1], tm), tok),
                  pl.BlockSpec((1, 3, d), lambda b, i: (b, 0, 0)),
                  pl.BlockSpec(onw.shape, const),
                  pl.BlockSpec(w_out_t.shape, const)],
        out_specs=pl.BlockSpec((1, tm, d), lambda b, i: (b, i, 0)),
        out_shape=jax.ShapeDtypeStruct((bsz, s, d), x.dtype),
        compiler_params=_params(("parallel", "parallel")),
        name="out_proj",
    )(x, ret_t, att_t, mg_t, mod3, onw, w_out_t)


def _score_bound_ok(qn2, kn2):
    bound2 = jnp.max(qn2, axis=(1, 3)) * jnp.max(kn2, axis=(1, 3))
    return (bound2 <= SCORE_BOUND_LOG2 ** 2).astype(jnp.int32).reshape(-1)


def _col(v):
    return v.astype(_F32).reshape(-1, 1)


def _block_sizes(s):
    tm = min(512, s)
    tq = min(512, s)
    tk = min(1024, s // ATTN_BUFS)
    chunk = min(256, s)
    return tm, tq, tk, chunk


def kernel(x, c, positions, norm_w, w_ada, b_ada, w_in, ret_decay_logit_fwd, ret_decay_logit_bwd, ret_gn_w, q_norm_w, w_uq, kv_norm_w, w_ukv, qn_nope_w, qn_rope_w, kn_nope_w, kn_rope_w, mla_out_norm_w, w_out):
    bsz, s, d = x.shape
    tm, tq, tk, chunk = _block_sizes(s)

    w_in_t = w_in.T.astype(_BF16)
    wq_t = w_uq.T.astype(_BF16)
    wkv_t = w_ukv.T.astype(_BF16)
    w_out_t = w_out.T.astype(_BF16)
    inv64 = (ROPE_BASE ** (-jnp.arange(0, RET_QK, 2, dtype=_F32) / RET_QK)).reshape(-1, 1)
    inv32 = (ROPE_BASE ** (-jnp.arange(0, MLA_ROPE, 2, dtype=_F32) / MLA_ROPE)).reshape(-1, 1)

    mod3 = _ada(c, w_ada, b_ada).reshape(bsz, 3, d)
    pos3 = positions.reshape(bsz, 1, s)

    (rq_t, rk_t, rv_t, rg_t, mg_t, qt, k, vt, qn2, kn2) = _inproj(
        x, pos3, mod3, norm_w, w_in_t, inv64, inv32,
        _col(q_norm_w), _col(kv_norm_w), _col(kn_rope_w), wq_t, wkv_t,
        _col(qn_nope_w), _col(qn_rope_w), _col(kn_nope_w), tm)

    ret_t = _retention(ret_decay_logit_fwd.astype(_F32), ret_decay_logit_bwd.astype(_F32),
                       rq_t, rk_t, rv_t, rg_t,
                       ret_gn_w.astype(_F32).reshape(RET_HEADS, RET_V, 1), chunk)

    att_t = _attention(_score_bound_ok(qn2, kn2), qt, k, vt, tq, tk)

    return _out_proj(x, ret_t, att_t, mg_t, mod3, _col(mla_out_norm_w), w_out_t, tm)
```

```python
import functools
import math

import jax
import jax.numpy as jnp
from jax import lax
from jax.experimental import pallas as pl
from jax.experimental.pallas import tpu as pltpu

RET_HEADS = 8
RET_QK = 64
RET_V = 128
MLA_HEADS = 8
MLA_NOPE = 64
MLA_ROPE = 32
MLA_V = 128
Q_LORA = 384
KV_LORA = 256
ROPE_BASE = 10000.0
EPS = 1e-6
HEAD_PAD = 128
LOG2E = 1.4426950408889634

_SIZES = (RET_HEADS * RET_QK, RET_HEADS * RET_QK, RET_HEADS * RET_V, RET_HEADS * RET_V,
          Q_LORA, KV_LORA, MLA_ROPE, MLA_HEADS * MLA_V)
_OFFS = tuple(sum(_SIZES[:i]) for i in range(len(_SIZES) + 1))

VMEM_LIMIT = 56 * 1024 * 1024

_NT = (((1,), (1,)), ((), ()))
_TN = (((0,), (0,)), ((), ()))
_F32 = jnp.float32
_BF16 = jnp.bfloat16


def _silu(v):
    u = 0.5 * v
    return u + u * jnp.tanh(u)


def _params(sem):
    return pltpu.CompilerParams(dimension_semantics=sem, vmem_limit_bytes=VMEM_LIMIT)


def _ada_kernel(c_ref, w_ref, b_ref, o_ref):
    a = _silu(c_ref[...]).astype(_BF16)
    o_ref[...] = jnp.dot(a, w_ref[...].astype(_BF16), preferred_element_type=_F32) + b_ref[...]


def _ada(c, w_ada, b_ada):
    bsz, d = c.shape
    n = w_ada.shape[1]
    tn = 1024
    return pl.pallas_call(
        _ada_kernel,
        grid=(n // tn,),
        in_specs=[pl.BlockSpec((bsz, d), lambda j: (0, 0)),
                  pl.BlockSpec((d, tn), lambda j: (0, j)),
                  pl.BlockSpec((1, tn), lambda j: (0, j))],
        out_specs=pl.BlockSpec((bsz, tn), lambda j: (0, j)),
        out_shape=jax.ShapeDtypeStruct((bsz, n), _F32),
        compiler_params=_params(("arbitrary",)),
        name="ada_mod",
    )(c, w_ada, b_ada.reshape(1, n))


def _inproj_kernel(x_ref, pos_ref, mod_ref, nw_ref, w_ref, inv64_ref, inv32_ref,
                   qnw_ref, kvnw_ref, krw_ref, wq_ref, wkv_ref, qnn_ref, qrn_ref, knn_ref,
                   rq_ref, rk_ref, rv_ref, rg_ref, mg_ref,
                   qt_ref, k_ref, vt_ref):
    tm = x_ref.shape[1]
    x = x_ref[0]
    shift = mod_ref[0, 0:1, :]
    scale = mod_ref[0, 1:2, :]
    gain = nw_ref[...] * (1.0 + scale)
    h = (x * lax.rsqrt(jnp.mean(x * x, axis=-1, keepdims=True) + EPS) * gain + shift).astype(_BF16)

    def proj(i, j=None):
        return lax.dot_general(w_ref[_OFFS[i]:_OFFS[i + 1 if j is None else j], :], h, _NT,
                               preferred_element_type=_F32)

    def rms(v, w_col):
        return v * lax.rsqrt(jnp.mean(v * v, axis=0, keepdims=True) + EPS) * w_col

    def rot(v, cos, sin):
        n = v.shape[0] // 2
        return v[:n] * cos - v[n:] * sin, v[n:] * cos + v[:n] * sin

    pos = pos_ref[0].astype(_F32)
    ang64 = inv64_ref[...] * pos
    cos64, sin64 = jnp.cos(ang64), jnp.sin(ang64)
    ang32 = inv32_ref[...] * pos
    cos32, sin32 = jnp.cos(ang32), jnp.sin(ang32)

    lat = proj(4, 7)
    cq = rms(lat[:Q_LORA], qnw_ref[...]).astype(_BF16)
    ckv = rms(lat[Q_LORA:Q_LORA + KV_LORA], kvnw_ref[...]).astype(_BF16)
    kr = jnp.concatenate(rot(rms(lat[Q_LORA + KV_LORA:], krw_ref[...]), cos32, sin32), axis=0)
    q_all = jnp.dot(wq_ref[...], cq, preferred_element_type=_F32)
    kv_all = jnp.dot(wkv_ref[...], ckv, preferred_element_type=_F32)
    zpad = jnp.zeros((HEAD_PAD - MLA_NOPE - MLA_ROPE, tm), _F32)
    hq = MLA_NOPE + MLA_ROPE
    hkv = MLA_NOPE + MLA_V
    for hd in range(MLA_HEADS):
        qn = rms(q_all[hd * hq: hd * hq + MLA_NOPE], qnn_ref[...])
        qr = rot(rms(q_all[hd * hq + MLA_NOPE: (hd + 1) * hq], qrn_ref[...]), cos32, sin32)
        qt_ref[0, hd] = jnp.concatenate([qn, qr[0], qr[1], zpad], axis=0).astype(_BF16)
        kn = rms(kv_all[hd * hkv: hd * hkv + MLA_NOPE], knn_ref[...])
        kh = jnp.concatenate([kn, kr, zpad], axis=0)
        k_ref[0, hd] = kh.T.astype(_BF16)
        vt_ref[0, hd] = kv_all[hd * hkv + MLA_NOPE: (hd + 1) * hkv].astype(_BF16)

    rg_ref[0] = _silu(proj(3)).astype(_BF16)
    mg_ref[0] = _silu(proj(7)).astype(_BF16)
    qk = proj(0, 2)
    for src, dst, mul in ((0, rq_ref, 1.0), (1, rk_ref, RET_QK ** -0.5)):
        for hd in range(RET_HEADS):
            lo = _OFFS[src] + hd * RET_QK
            r1, r2 = rot(qk[lo: lo + RET_QK], cos64, sin64)
            dst[0, hd * RET_QK: hd * RET_QK + RET_QK // 2, :] = (r1 * mul).astype(_BF16)
            dst[0, hd * RET_QK + RET_QK // 2: (hd + 1) * RET_QK, :] = (r2 * mul).astype(_BF16)
    rv_ref[0] = proj(2).astype(_BF16)


def _inproj(x, pos3, mod3, norm_w, w_in_t, inv64, inv32, qnw, kvnw, krw, wq_t, wkv_t, qnn, qrn, knn, tm):
    bsz, s, d = x.shape
    const = lambda b, i: (0, 0)
    tok = lambda b, i: (b, 0, i)
    head_t = lambda b, i: (b, 0, 0, i)
    sizes = (_SIZES[0], _SIZES[1], _SIZES[2], _SIZES[3], _SIZES[7])
    out_shape = [jax.ShapeDtypeStruct((bsz, n, s), _BF16) for n in sizes]
    out_specs = [pl.BlockSpec((1, n, tm), tok) for n in sizes]
    out_shape += [jax.ShapeDtypeStruct((bsz, MLA_HEADS, HEAD_PAD, s), _BF16),
                  jax.ShapeDtypeStruct((bsz, MLA_HEADS, s, HEAD_PAD), _BF16),
                  jax.ShapeDtypeStruct((bsz, MLA_HEADS, MLA_V, s), _BF16)]
    out_specs += [pl.BlockSpec((1, MLA_HEADS, HEAD_PAD, tm), head_t),
                  pl.BlockSpec((1, MLA_HEADS, tm, HEAD_PAD), lambda b, i: (b, 0, i, 0)),
                  pl.BlockSpec((1, MLA_HEADS, MLA_V, tm), head_t)]
    consts = (w_in_t, inv64, inv32, qnw, kvnw, krw, wq_t, wkv_t, qnn, qrn, knn)
    return pl.pallas_call(
        _inproj_kernel,
        grid=(bsz, s // tm),
        in_specs=[pl.BlockSpec((1, tm, d), lambda b, i: (b, i, 0)),
                  pl.BlockSpec((1, 1, tm), tok),
                  pl.BlockSpec((1, 3, d), lambda b, i: (b, 0, 0)),
                  pl.BlockSpec((1, d), const)] + [pl.BlockSpec(a.shape, const) for a in consts],
        out_specs=out_specs,
        out_shape=out_shape,
        compiler_params=_params(("parallel", "parallel")),
        name="in_proj",
    )(x, pos3, mod3, norm_w.reshape(1, d), *consts)


def _log_sigmoid(v):
    return jnp.minimum(v, 0.0) - jnp.log(1.0 + jnp.exp(-jnp.abs(v)))


def _retention_kernel(lf_ref, lb_ref, q_ref, k_ref, v_ref, g_ref, gnw_ref, o_ref,
                      kv_ref, st_ref, *, chunk, n_chunks):
    hd = pl.program_id(1)
    c = chunk
    lgf = _log_sigmoid(jnp.full((1, 1), lf_ref[hd], _F32))
    lgb = _log_sigmoid(jnp.full((1, 1), lb_ref[hd], _F32))

    pos_l = lax.broadcasted_iota(jnp.int32, (1, c), 1).astype(_F32)
    kdec_f = jnp.exp(lgf * (c - 1.0 - pos_l))
    kdec_b = jnp.exp(lgb * pos_l)
    qdec_f = jnp.exp(lgf * (pos_l + 1.0))
    qdec_b = jnp.exp(lgb * (c - pos_l))
    jj = lax.broadcasted_iota(jnp.int32, (c, c), 0).astype(_F32)
    ii = lax.broadcasted_iota(jnp.int32, (c, c), 1).astype(_F32)
    dmat = jnp.where(ii >= jj, jnp.exp(lgf * jnp.maximum(ii - jj, 0.0)),
                     jnp.exp(lgb * jnp.maximum(jj - ii, 0.0)))
    lane = lax.broadcasted_iota(jnp.int32, (1, 2 * RET_QK), 1)
    is_f = lane < RET_QK
    sdec = jnp.where(is_f, jnp.exp(lgf * c), jnp.exp(lgb * c))

    def chunk_ds(i):
        return pl.ds(pl.multiple_of(i * c, c), c)

    def kv_body(i, carry):
        ds = chunk_ds(i)
        kc = k_ref[0, :, ds].astype(_F32)
        kfb = jnp.concatenate([kc * kdec_f, kc * kdec_b], axis=0).astype(_BF16)
        kv_ref[i] = lax.dot_general(v_ref[0, :, ds], kfb, _NT, preferred_element_type=_F32)
        return carry

    lax.fori_loop(0, n_chunks, kv_body, 0, unroll=16)

    def bwd_body(t, sb):
        i = n_chunks - 1 - t
        st_ref[i] = sb
        return sb * sdec + kv_ref[i]

    lax.fori_loop(0, n_chunks, bwd_body, jnp.zeros((RET_V, 2 * RET_QK), _F32))

    gnw = gnw_ref[0]

    def out_body(i, sf):
        ds = chunk_ds(i)
        qc = q_ref[0, :, ds]
        kc = k_ref[0, :, ds]
        st = lax.dot_general(kc, qc, _TN, preferred_element_type=_F32)
        pt = (st * dmat).astype(_BF16)
        qf = qc.astype(_F32)
        qfb = jnp.concatenate([qf * qdec_f, qf * qdec_b], axis=0).astype(_BF16)
        state = jnp.where(is_f, sf, st_ref[i]).astype(_BF16)
        o = (jnp.dot(v_ref[0, :, ds], pt, preferred_element_type=_F32)
             + jnp.dot(state, qfb, preferred_element_type=_F32))
        mu = jnp.mean(o, axis=0, keepdims=True)
        dlt = o - mu
        var = jnp.mean(dlt * dlt, axis=0, keepdims=True)
        on = dlt * lax.rsqrt(var + EPS) * gnw
        o_ref[0, :, ds] = (on * g_ref[0, :, ds].astype(_F32)).astype(_BF16)
        return sf * sdec + kv_ref[i]

    lax.fori_loop(0, n_chunks, out_body, jnp.zeros((RET_V, 2 * RET_QK), _F32), unroll=16)


def _retention(lf, lb, rq_t, rk_t, rv_t, rg_t, gnw3, chunk):
    bsz, _, s = rq_t.shape
    n_chunks = s // chunk
    smem = pl.BlockSpec(memory_space=pltpu.SMEM)
    return pl.pallas_call(
        functools.partial(_retention_kernel, chunk=chunk, n_chunks=n_chunks),
        grid=(bsz, RET_HEADS),
        in_specs=[smem, smem,
                  pl.BlockSpec((1, RET_QK, s), lambda b, h: (b, h, 0)),
                  pl.BlockSpec((1, RET_QK, s), lambda b, h: (b, h, 0)),
                  pl.BlockSpec((1, RET_V, s), lambda b, h: (b, h, 0)),
                  pl.BlockSpec((1, RET_V, s), lambda b, h: (b, h, 0)),
                  pl.BlockSpec((1, RET_V, 1), lambda b, h: (h, 0, 0))],
        out_specs=pl.BlockSpec((1, RET_V, s), lambda b, h: (b, h, 0)),
        out_shape=jax.ShapeDtypeStruct((bsz, RET_HEADS * RET_V, s), _BF16),
        scratch_shapes=[pltpu.VMEM((n_chunks, RET_V, 2 * RET_QK), _F32),
                        pltpu.VMEM((n_chunks, RET_V, 2 * RET_QK), _F32)],
        compiler_params=_params(("parallel", "parallel")),
        name="retention",
    )(lf, lb, rq_t, rk_t, rv_t, rg_t, gnw3)


def _attn_kernel(fast_ref, qt_ref, k_ref, vt_ref, o_ref, *bufs, tq, tk, tqf, tkf, n_q, n_kv):
    nb = len(bufs)
    fast = fast_ref[0]

    def q_ds(qi):
        return pl.ds(pl.multiple_of(qi * tq, tq), tq)

    def kv_ds(j):
        return pl.ds(pl.multiple_of(j * tk, tk), tk)

    def write(qi, l, acc):
        o_ref[0, :, q_ds(qi)] = (acc * (1.0 / l)).astype(_BF16)

    @pl.when(fast != 0)
    def _fast_path():
        def body(qi, carry):
            qds = pl.ds(pl.multiple_of(qi * tqf, tqf), tqf)
            qt = qt_ref[0, 0, :, qds]
            l = jnp.zeros((1, tqf), _F32)
            acc = jnp.zeros((MLA_V, tqf), _F32)
            for j in range(n_kv * tk // tkf):
                ds = pl.ds(j * tkf, tkf)
                st = jnp.dot(k_ref[0, 0, ds, :], qt, preferred_element_type=_F32)
                p = jnp.exp2(st)
                l = l + jnp.sum(p, axis=0, keepdims=True)
                acc = acc + jnp.dot(vt_ref[0, 0, :, ds], p.astype(_BF16),
                                    preferred_element_type=_F32)
            o_ref[0, :, qds] = (acc * (1.0 / l)).astype(_BF16)
            return carry

        lax.fori_loop(0, n_q * tq // tqf, body, 0)

    @pl.when(fast == 0)
    def _safe_path():
        def scores(qi, j, m, s_ref):
            st = jnp.dot(k_ref[0, 0, kv_ds(j), :], qt_ref[0, 0, :, q_ds(qi)],
                         preferred_element_type=_F32)
            s_ref[...] = st
            return jnp.maximum(m, jnp.max(st, axis=0, keepdims=True))

        def accumulate(j, m_prev, m, l, acc, s_ref):
            alpha = jnp.exp2(m_prev - m)
            p = jnp.exp2(s_ref[...] - m)
            l = alpha * l + jnp.sum(p, axis=0, keepdims=True)
            acc = alpha * acc + jnp.dot(vt_ref[0, 0, :, kv_ds(j)], p.astype(_BF16),
                                        preferred_element_type=_F32)
            return l, acc

        neg_inf = jnp.full((1, tq), -jnp.inf, _F32)

        def body(qi, carry):
            ms = [neg_inf] + list(carry)
            nxt = []
            qn = jnp.minimum(qi + 1, n_q - 1)
            l = jnp.zeros((1, tq), _F32)
            acc = jnp.zeros((MLA_V, tq), _F32)
            for j in range(n_kv):
                t = j + ATTN_AHEAD
                if t < n_kv:
                    ms.append(scores(qi, t, ms[-1], bufs[t % nb]))
                else:
                    nxt.append(scores(qn, t - n_kv, nxt[-1] if nxt else neg_inf, bufs[t % nb]))
                l, acc = accumulate(j, ms[j], ms[j + 1], l, acc, bufs[j % nb])
            write(qi, l, acc)
            return tuple(nxt)

        first = []
        for j in range(ATTN_AHEAD):
            first.append(scores(0, j, first[-1] if first else neg_inf, bufs[j]))
        lax.fori_loop(0, n_q, body, tuple(first))


ATTN_AHEAD = 2
ATTN_BUFS = 4
SCORE_BOUND_LOG2 = 60.0
ATTN_FAST_TK = 4096
ATTN_FAST_TQ = 1024


def _attention(fast, qt, k, vt, tq, tk):
    bsz, nh, _, s = qt.shape
    n_kv = s // tk
    assert s % tq == 0 and s % tk == 0 and n_kv % ATTN_BUFS == 0 and ATTN_AHEAD < ATTN_BUFS
    return pl.pallas_call(
        functools.partial(_attn_kernel, tq=tq, tk=tk, tqf=min(ATTN_FAST_TQ, s), tkf=min(ATTN_FAST_TK, s),
                          n_q=s // tq, n_kv=s // tk),
        grid=(bsz, nh),
        in_specs=[pl.BlockSpec(memory_space=pltpu.SMEM),
                  pl.BlockSpec((1, 1, HEAD_PAD, s), lambda b, h: (b, h, 0, 0)),
                  pl.BlockSpec((1, 1, s, HEAD_PAD), lambda b, h: (b, h, 0, 0)),
                  pl.BlockSpec((1, 1, MLA_V, s), lambda b, h: (b, h, 0, 0))],
        out_specs=pl.BlockSpec((1, MLA_V, s), lambda b, h: (b, h, 0)),
        out_shape=jax.ShapeDtypeStruct((bsz, nh * MLA_V, s), _BF16),
        scratch_shapes=[pltpu.VMEM((tk, tq), _F32)] * ATTN_BUFS,
        compiler_params=_params(("parallel", "parallel")),
        name="mla_attention",
    )(fast, qt, k, vt)


OUT_BLOCK = 256


def _out_kernel(x_ref, ret_ref, att_ref, mg_ref, mod_ref, onw_ref, w_ref, o_ref):
    att = att_ref[0].astype(_F32)
    mla = att * lax.rsqrt(jnp.mean(att * att, axis=0, keepdims=True) + EPS) * onw_ref[...]
    mla = (mla * mg_ref[0].astype(_F32)).astype(_BF16)
    cat = jnp.concatenate([ret_ref[0], mla], axis=0)
    d = w_ref.shape[0]
    for lo in range(0, d, OUT_BLOCK):
        yt = jnp.dot(w_ref[lo:lo + OUT_BLOCK, :], cat, preferred_element_type=_F32)
        gate = mod_ref[0, 2:3, lo:lo + OUT_BLOCK]
        o_ref[0, :, lo:lo + OUT_BLOCK] = x_ref[0, :, lo:lo + OUT_BLOCK] + gate * yt.T


def _out_proj(x, ret_t, att_t, mg_t, mod3, onw, w_out_t, tm):
    bsz, s, d = x.shape
    tok = lambda b, i: (b, 0, i)
    const = lambda b, i: (0, 0)
    return pl.pallas_call(
        _out_kernel,
        grid=(bsz, s // tm),
        in_specs=[pl.BlockSpec((1, tm, d), lambda b, i: (b, i, 0)),
                  pl.BlockSpec((1, ret_t.shape[1], tm), tok),
                  pl.BlockSpec((1, att_t.shape[1], tm), tok),
                  pl.BlockSpec((1, mg_t.shape[1], tm), tok),
                  pl.BlockSpec((1, 3, d), lambda b, i: (b, 0, 0)),
                  pl.BlockSpec(onw.shape, const),
                  pl.BlockSpec(w_out_t.shape, const)],
        out_specs=pl.BlockSpec((1, tm, d), lambda b, i: (b, i, 0)),
        out_shape=jax.ShapeDtypeStruct((bsz, s, d), x.dtype),
        compiler_params=_params(("parallel", "parallel")),
        name="out_proj",
    )(x, ret_t, att_t, mg_t, mod3, onw, w_out_t)


def _score_bound_ok(qn_w, qr_w, kn_w, kr_w):
    def sq(w, d):
        return d * jnp.max(jnp.square(w.astype(_F32)))
    bound2 = (sq(qn_w, MLA_NOPE) + sq(qr_w, MLA_ROPE)) * (sq(kn_w, MLA_NOPE) + sq(kr_w, MLA_ROPE))
    return (bound2 <= SCORE_BOUND_LOG2 ** 2).astype(jnp.int32).reshape(1)


def _col(v):
    return v.astype(_F32).reshape(-1, 1)


def _block_sizes(s):
    tm = min(512, s)
    tq = min(512, s)
    tk = min(1024, s // ATTN_BUFS)
    chunk = min(256, s)
    return tm, tq, tk, chunk


def kernel(x, c, positions, norm_w, w_ada, b_ada, w_in, ret_decay_logit_fwd, ret_decay_logit_bwd, ret_gn_w, q_norm_w, w_uq, kv_norm_w, w_ukv, qn_nope_w, qn_rope_w, kn_nope_w, kn_rope_w, mla_out_norm_w, w_out):
    bsz, s, d = x.shape
    tm, tq, tk, chunk = _block_sizes(s)

    w_in_t = w_in.T.astype(_BF16)
    wq_t = w_uq.T.astype(_BF16)
    wkv_t = w_ukv.T.astype(_BF16)
    w_out_t = w_out.T.astype(_BF16)
    inv64 = (ROPE_BASE ** (-jnp.arange(0, RET_QK, 2, dtype=_F32) / RET_QK)).reshape(-1, 1)
    inv32 = (ROPE_BASE ** (-jnp.arange(0, MLA_ROPE, 2, dtype=_F32) / MLA_ROPE)).reshape(-1, 1)

    mod3 = _ada(c, w_ada, b_ada).reshape(bsz, 3, d)
    pos3 = positions.reshape(bsz, 1, s)

    q_mul = (MLA_NOPE + MLA_ROPE) ** -0.5 * LOG2E
    qn_w = qn_nope_w.astype(_F32) * q_mul
    qr_w = qn_rope_w.astype(_F32) * q_mul

    (rq_t, rk_t, rv_t, rg_t, mg_t, qt, k, vt) = _inproj(
        x, pos3, mod3, norm_w, w_in_t, inv64, inv32,
        _col(q_norm_w), _col(kv_norm_w), _col(kn_rope_w), wq_t, wkv_t,
        _col(qn_w), _col(qr_w), _col(kn_nope_w), tm)

    ret_t = _retention(ret_decay_logit_fwd.astype(_F32), ret_decay_logit_bwd.astype(_F32),
                       rq_t, rk_t, rv_t, rg_t,
                       ret_gn_w.astype(_F32).reshape(RET_HEADS, RET_V, 1), chunk)

    att_t = _attention(_score_bound_ok(qn_w, qr_w, kn_nope_w, kn_rope_w), qt, k, vt, tq, tk)

    return _out_proj(x, ret_t, att_t, mg_t, mod3, _col(mla_out_norm_w), w_out_t, tm)
```

```python
import functools
import math

import jax
import jax.numpy as jnp
from jax import lax
from jax.experimental import pallas as pl
from jax.experimental.pallas import tpu as pltpu

RET_HEADS = 8
RET_QK = 64
RET_V = 128
MLA_HEADS = 8
MLA_NOPE = 64
MLA_ROPE = 32
MLA_V = 128
Q_LORA = 384
KV_LORA = 256
ROPE_BASE = 10000.0
EPS = 1e-6
HEAD_PAD = 128
LOG2E = 1.4426950408889634

_SIZES = (RET_HEADS * RET_QK, RET_HEADS * RET_QK, RET_HEADS * RET_V, RET_HEADS * RET_V,
          Q_LORA, KV_LORA, MLA_ROPE, MLA_HEADS * MLA_V)
_OFFS = tuple(sum(_SIZES[:i]) for i in range(len(_SIZES) + 1))

VMEM_LIMIT = 56 * 1024 * 1024

_NT = (((1,), (1,)), ((), ()))
_TN = (((0,), (0,)), ((), ()))
_F32 = jnp.float32
_BF16 = jnp.bfloat16


def _silu(v):
    u = 0.5 * v
    return u + u * jnp.tanh(u)


def _params(sem):
    return pltpu.CompilerParams(dimension_semantics=sem, vmem_limit_bytes=VMEM_LIMIT)


def _ada_kernel(c_ref, w_ref, b_ref, o_ref):
    a = _silu(c_ref[...]).astype(_BF16)
    o_ref[...] = jnp.dot(a, w_ref[...].astype(_BF16), preferred_element_type=_F32) + b_ref[...]


def _ada(c, w_ada, b_ada):
    bsz, d = c.shape
    n = w_ada.shape[1]
    tn = 1024
    return pl.pallas_call(
        _ada_kernel,
        grid=(n // tn,),
        in_specs=[pl.BlockSpec((bsz, d), lambda j: (0, 0)),
                  pl.BlockSpec((d, tn), lambda j: (0, j)),
                  pl.BlockSpec((1, tn), lambda j: (0, j))],
        out_specs=pl.BlockSpec((bsz, tn), lambda j: (0, j)),
        out_shape=jax.ShapeDtypeStruct((bsz, n), _F32),
        compiler_params=_params(("arbitrary",)),
        name="ada_mod",
    )(c, w_ada, b_ada.reshape(1, n))


def _inproj_kernel(x_ref, pos_ref, mod_ref, nw_ref, w_ref, inv64_ref, inv32_ref,
                   qnw_ref, kvnw_ref, krw_ref, wq_ref, wkv_ref, qnn_ref, qrn_ref, knn_ref,
                   rq_ref, rk_ref, rv_ref, rg_ref, mg_ref,
                   qt_ref, k_ref, vt_ref):
    tm = x_ref.shape[1]
    x = x_ref[0]
    shift = mod_ref[0, 0:1, :]
    scale = mod_ref[0, 1:2, :]
    gain = nw_ref[...] * (1.0 + scale)
    h = (x * lax.rsqrt(jnp.mean(x * x, axis=-1, keepdims=True) + EPS) * gain + shift).astype(_BF16)

    def proj(i, j=None):
        return lax.dot_general(w_ref[_OFFS[i]:_OFFS[i + 1 if j is None else j], :], h, _NT,
                               preferred_element_type=_F32)

    def rms(v, w_col):
        return v * lax.rsqrt(jnp.mean(v * v, axis=0, keepdims=True) + EPS) * w_col

    def rot(v, cos, sin):
        n = v.shape[0] // 2
        return v[:n] * cos - v[n:] * sin, v[n:] * cos + v[:n] * sin

    pos = pos_ref[0].astype(_F32)
    ang64 = inv64_ref[...] * pos
    cos64, sin64 = jnp.cos(ang64), jnp.sin(ang64)
    ang32 = inv32_ref[...] * pos
    cos32, sin32 = jnp.cos(ang32), jnp.sin(ang32)

    lat = proj(4, 7)
    cq = rms(lat[:Q_LORA], qnw_ref[...]).astype(_BF16)
    ckv = rms(lat[Q_LORA:Q_LORA + KV_LORA], kvnw_ref[...]).astype(_BF16)
    kr = jnp.concatenate(rot(rms(lat[Q_LORA + KV_LORA:], krw_ref[...]), cos32, sin32), axis=0)
    q_all = jnp.dot(wq_ref[...], cq, preferred_element_type=_F32)
    kv_all = jnp.dot(wkv_ref[...], ckv, preferred_element_type=_F32)
    zpad = jnp.zeros((HEAD_PAD - MLA_NOPE - MLA_ROPE, tm), _F32)
    hq = MLA_NOPE + MLA_ROPE
    hkv = MLA_NOPE + MLA_V
    for hd in range(MLA_HEADS):
        qn = rms(q_all[hd * hq: hd * hq + MLA_NOPE], qnn_ref[...])
        qr = rot(rms(q_all[hd * hq + MLA_NOPE: (hd + 1) * hq], qrn_ref[...]), cos32, sin32)
        qt_ref[0, hd] = jnp.concatenate([qn, qr[0], qr[1], zpad], axis=0).astype(_BF16)
        kn = rms(kv_all[hd * hkv: hd * hkv + MLA_NOPE], knn_ref[...])
        kh = jnp.concatenate([kn, kr, zpad], axis=0)
        k_ref[0, hd] = kh.T.astype(_BF16)
        vt_ref[0, hd] = kv_all[hd * hkv + MLA_NOPE: (hd + 1) * hkv].astype(_BF16)

    rg_ref[0] = _silu(proj(3)).astype(_BF16)
    mg_ref[0] = _silu(proj(7)).astype(_BF16)
    qk = proj(0, 2)
    for src, dst, mul in ((0, rq_ref, 1.0), (1, rk_ref, RET_QK ** -0.5)):
        for hd in range(RET_HEADS):
            lo = _OFFS[src] + hd * RET_QK
            r1, r2 = rot(qk[lo: lo + RET_QK], cos64, sin64)
            dst[0, hd * RET_QK: hd * RET_QK + RET_QK // 2, :] = (r1 * mul).astype(_BF16)
            dst[0, hd * RET_QK + RET_QK // 2: (hd + 1) * RET_QK, :] = (r2 * mul).astype(_BF16)
    rv_ref[0] = proj(2).astype(_BF16)


def _inproj(x, pos3, mod3, norm_w, w_in_t, inv64, inv32, qnw, kvnw, krw, wq_t, wkv_t, qnn, qrn, knn, tm):
    bsz, s, d = x.shape
    const = lambda b, i: (0, 0)
    tok = lambda b, i: (b, 0, i)
    head_t = lambda b, i: (b, 0, 0, i)
    sizes = (_SIZES[0], _SIZES[1], _SIZES[2], _SIZES[3], _SIZES[7])
    out_shape = [jax.ShapeDtypeStruct((bsz, n, s), _BF16) for n in sizes]
    out_specs = [pl.BlockSpec((1, n, tm), tok) for n in sizes]
    out_shape += [jax.ShapeDtypeStruct((bsz, MLA_HEADS, HEAD_PAD, s), _BF16),
                  jax.ShapeDtypeStruct((bsz, MLA_HEADS, s, HEAD_PAD), _BF16),
                  jax.ShapeDtypeStruct((bsz, MLA_HEADS, MLA_V, s), _BF16)]
    out_specs += [pl.BlockSpec((1, MLA_HEADS, HEAD_PAD, tm), head_t),
                  pl.BlockSpec((1, MLA_HEADS, tm, HEAD_PAD), lambda b, i: (b, 0, i, 0)),
                  pl.BlockSpec((1, MLA_HEADS, MLA_V, tm), head_t)]
    consts = (w_in_t, inv64, inv32, qnw, kvnw, krw, wq_t, wkv_t, qnn, qrn, knn)
    return pl.pallas_call(
        _inproj_kernel,
        grid=(bsz, s // tm),
        in_specs=[pl.BlockSpec((1, tm, d), lambda b, i: (b, i, 0)),
                  pl.BlockSpec((1, 1, tm), tok),
                  pl.BlockSpec((1, 3, d), lambda b, i: (b, 0, 0)),
                  pl.BlockSpec((1, d), const)] + [pl.BlockSpec(a.shape, const) for a in consts],
        out_specs=out_specs,
        out_shape=out_shape,
        compiler_params=_params(("parallel", "parallel")),
        name="in_proj",
    )(x, pos3, mod3, norm_w.reshape(1, d), *consts)


def _log_sigmoid(v):
    return jnp.minimum(v, 0.0) - jnp.log(1.0 + jnp.exp(-jnp.abs(v)))


def _retention_kernel(lf_ref, lb_ref, q_ref, k_ref, v_ref, g_ref, gnw_ref, o_ref,
                      kv_ref, st_ref, *, chunk, n_chunks):
    hd = pl.program_id(1)
    c = chunk
    lgf = _log_sigmoid(jnp.full((1, 1), lf_ref[hd], _F32))
    lgb = _log_sigmoid(jnp.full((1, 1), lb_ref[hd], _F32))

    pos_l = lax.broadcasted_iota(jnp.int32, (1, c), 1).astype(_F32)
    kdec_f = jnp.exp(lgf * (c - 1.0 - pos_l))
    kdec_b = jnp.exp(lgb * pos_l)
    qdec_f = jnp.exp(lgf * (pos_l + 1.0))
    qdec_b = jnp.exp(lgb * (c - pos_l))
    jj = lax.broadcasted_iota(jnp.int32, (c, c), 0).astype(_F32)
    ii = lax.broadcasted_iota(jnp.int32, (c, c), 1).astype(_F32)
    dmat = jnp.where(ii >= jj, jnp.exp(lgf * jnp.maximum(ii - jj, 0.0)),
                     jnp.exp(lgb * jnp.maximum(jj - ii, 0.0)))
    lane = lax.broadcasted_iota(jnp.int32, (1, 2 * RET_QK), 1)
    is_f = lane < RET_QK
    sdec = jnp.where(is_f, jnp.exp(lgf * c), jnp.exp(lgb * c))

    def chunk_ds(i):
        return pl.ds(pl.multiple_of(i * c, c), c)

    def kv_body(i, carry):
        ds = chunk_ds(i)
        kc = k_ref[0, :, ds].astype(_F32)
        kfb = jnp.concatenate([kc * kdec_f, kc * kdec_b], axis=0).astype(_BF16)
        kv_ref[i] = lax.dot_general(v_ref[0, :, ds], kfb, _NT, preferred_element_type=_F32)
        return carry

    lax.fori_loop(0, n_chunks, kv_body, 0, unroll=16)

    def bwd_body(t, sb):
        i = n_chunks - 1 - t
        st_ref[i] = sb
        return sb * sdec + kv_ref[i]

    lax.fori_loop(0, n_chunks, bwd_body, jnp.zeros((RET_V, 2 * RET_QK), _F32))

    gnw = gnw_ref[0]

    def out_body(i, sf):
        ds = chunk_ds(i)
        qc = q_ref[0, :, ds]
        kc = k_ref[0, :, ds]
        st = lax.dot_general(kc, qc, _TN, preferred_element_type=_F32)
        pt = (st * dmat).astype(_BF16)
        qf = qc.astype(_F32)
        qfb = jnp.concatenate([qf * qdec_f, qf * qdec_b], axis=0).astype(_BF16)
        state = jnp.where(is_f, sf, st_ref[i]).astype(_BF16)
        o = (jnp.dot(v_ref[0, :, ds], pt, preferred_element_type=_F32)
             + jnp.dot(state, qfb, preferred_element_type=_F32))
        mu = jnp.mean(o, axis=0, keepdims=True)
        dlt = o - mu
        var = jnp.mean(dlt * dlt, axis=0, keepdims=True)
        on = dlt * lax.rsqrt(var + EPS) * gnw
        o_ref[0, :, ds] = (on * g_ref[0, :, ds].astype(_F32)).astype(_BF16)
        return sf * sdec + kv_ref[i]

    lax.fori_loop(0, n_chunks, out_body, jnp.zeros((RET_V, 2 * RET_QK), _F32), unroll=16)


def _retention(lf, lb, rq_t, rk_t, rv_t, rg_t, gnw3, chunk):
    bsz, _, s = rq_t.shape
    n_chunks = s // chunk
    smem = pl.BlockSpec(memory_space=pltpu.SMEM)
    return pl.pallas_call(
        functools.partial(_retention_kernel, chunk=chunk, n_chunks=n_chunks),
        grid=(bsz, RET_HEADS),
        in_specs=[smem, smem,
                  pl.BlockSpec((1, RET_QK, s), lambda b, h: (b, h, 0)),
                  pl.BlockSpec((1, RET_QK, s), lambda b, h: (b, h, 0)),
                  pl.BlockSpec((1, RET_V, s), lambda b, h: (b, h, 0)),
                  pl.BlockSpec((1, RET_V, s), lambda b, h: (b, h, 0)),
                  pl.BlockSpec((1, RET_V, 1), lambda b, h: (h, 0, 0))],
        out_specs=pl.BlockSpec((1, RET_V, s), lambda b, h: (b, h, 0)),
        out_shape=jax.ShapeDtypeStruct((bsz, RET_HEADS * RET_V, s), _BF16),
        scratch_shapes=[pltpu.VMEM((n_chunks, RET_V, 2 * RET_QK), _F32),
                        pltpu.VMEM((n_chunks, RET_V, 2 * RET_QK), _F32)],
        compiler_params=_params(("parallel", "parallel")),
        name="retention",
    )(lf, lb, rq_t, rk_t, rv_t, rg_t, gnw3)


def _attn_kernel(fast_ref, qt_ref, k_ref, vt_ref, o_ref, *bufs, tq, tk, tqf, tkf, n_q, n_kv):
    nb = len(bufs)
    fast = fast_ref[0]

    def q_ds(qi):
        return pl.ds(pl.multiple_of(qi * tq, tq), tq)

    def kv_ds(j):
        return pl.ds(pl.multiple_of(j * tk, tk), tk)

    def write(qi, l, acc):
        o_ref[0, :, q_ds(qi)] = (acc * (1.0 / l)).astype(_BF16)

    @pl.when(fast != 0)
    def _fast_path():
        def body(qi, carry):
            qds = pl.ds(pl.multiple_of(qi * tqf, tqf), tqf)
            qt = qt_ref[0, 0, :, qds]
            l = jnp.zeros((1, tqf), _F32)
            acc = jnp.zeros((MLA_V, tqf), _F32)
            for j in range(n_kv * tk // tkf):
                ds = pl.ds(j * tkf, tkf)
                st = jnp.dot(k_ref[0, 0, ds, :], qt, preferred_element_type=_F32)
                p = jnp.exp2(st)
                l = l + jnp.sum(p, axis=0, keepdims=True)
                acc = acc + jnp.dot(vt_ref[0, 0, :, ds], p.astype(_BF16),
                                    preferred_element_type=_F32)
            o_ref[0, :, qds] = (acc * (1.0 / l)).astype(_BF16)
            return carry

        lax.fori_loop(0, n_q * tq // tqf, body, 0)

    @pl.when(fast == 0)
    def _safe_path():
        def scores(qi, j, m, s_ref):
            st = jnp.dot(k_ref[0, 0, kv_ds(j), :], qt_ref[0, 0, :, q_ds(qi)],
                         preferred_element_type=_F32)
            s_ref[...] = st
            return jnp.maximum(m, jnp.max(st, axis=0, keepdims=True))

        def accumulate(j, m_prev, m, l, acc, s_ref):
            alpha = jnp.exp2(m_prev - m)
            p = jnp.exp2(s_ref[...] - m)
            l = alpha * l + jnp.sum(p, axis=0, keepdims=True)
            acc = alpha * acc + jnp.dot(vt_ref[0, 0, :, kv_ds(j)], p.astype(_BF16),
                                        preferred_element_type=_F32)
            return l, acc

        neg_inf = jnp.full((1, tq), -jnp.inf, _F32)

        def body(qi, carry):
            ms = [neg_inf] + list(carry)
            nxt = []
            qn = jnp.minimum(qi + 1, n_q - 1)
            l = jnp.zeros((1, tq), _F32)
            acc = jnp.zeros((MLA_V, tq), _F32)
            for j in range(n_kv):
                t = j + ATTN_AHEAD
                if t < n_kv:
                    ms.append(scores(qi, t, ms[-1], bufs[t % nb]))
                else:
                    nxt.append(scores(qn, t - n_kv, nxt[-1] if nxt else neg_inf, bufs[t % nb]))
                l, acc = accumulate(j, ms[j], ms[j + 1], l, acc, bufs[j % nb])
            write(qi, l, acc)
            return tuple(nxt)

        first = []
        for j in range(ATTN_AHEAD):
            first.append(scores(0, j, first[-1] if first else neg_inf, bufs[j]))
        lax.fori_loop(0, n_q, body, tuple(first))


ATTN_AHEAD = 2
ATTN_BUFS = 4
SCORE_BOUND_LOG2 = 60.0
ATTN_FAST_TK = 4096
ATTN_FAST_TQ = 1024


def _attention(fast, qt, k, vt, tq, tk):
    bsz, nh, _, s = qt.shape
    n_kv = s // tk
    assert s % tq == 0 and s % tk == 0 and n_kv % ATTN_BUFS == 0 and ATTN_AHEAD < ATTN_BUFS
    return pl.pallas_call(
        functools.partial(_attn_kernel, tq=tq, tk=tk, tqf=min(ATTN_FAST_TQ, s), tkf=min(ATTN_FAST_TK, s),
                          n_q=s // tq, n_kv=s // tk),
        grid=(bsz, nh),
        in_specs=[pl.BlockSpec(memory_space=pltpu.SMEM),
                  pl.BlockSpec((1, 1, HEAD_PAD, s), lambda b, h: (b, h, 0, 0)),
                  pl.BlockSpec((1, 1, s, HEAD_PAD), lambda b, h: (b, h, 0, 0)),
                  pl.BlockSpec((1, 1, MLA_V, s), lambda b, h: (b, h, 0, 0))],
        out_specs=pl.BlockSpec((1, MLA_V, s), lambda b, h: (b, h, 0)),
        out_shape=jax.ShapeDtypeStruct((bsz, nh * MLA_V, s), _BF16),
        scratch_shapes=[pltpu.VMEM((tk, tq), _F32)] * ATTN_BUFS,
        compiler_params=_params(("parallel", "parallel")),
        name="mla_attention",
    )(fast, qt, k, vt)


OUT_BLOCK = 256


def _out_kernel(x_ref, ret_ref, att_ref, mg_ref, mod_ref, onw_ref, w_ref, o_ref):
    att = att_ref[0].astype(_F32)
    mla = att * lax.rsqrt(jnp.mean(att * att, axis=0, keepdims=True) + EPS) * onw_ref[...]
    mla = (mla * mg_ref[0].astype(_F32)).astype(_BF16)
    cat = jnp.concatenate([ret_ref[0], mla], axis=0)
    d = w_ref.shape[0]
    for lo in range(0, d, OUT_BLOCK):
        yt = jnp.dot(w_ref[lo:lo + OUT_BLOCK, :], cat, preferred_element_type=_F32)
        gate = mod_ref[0, 2:3, lo:lo + OUT_BLOCK]
        o_ref[0, :, lo:lo + OUT_BLOCK] = x_ref[0, :, lo:lo + OUT_BLOCK] + gate * yt.T


def _out_proj(x, ret_t, att_t, mg_t, mod3, onw, w_out_t, tm):
    bsz, s, d = x.shape
    tok = lambda b, i: (b, 0, i)
    const = lambda b, i: (0, 0)
    return pl.pallas_call(
        _out_kernel,
        grid=(bsz, s // tm),
        in_specs=[pl.BlockSpec((1, tm, d), lambda b, i: (b, i, 0)),
                  pl.BlockSpec((1, ret_t.shape[1], tm), tok),
                  pl.BlockSpec((1, att_t.shape[1], tm), tok),
                  pl.BlockSpec((1, mg_t.shape[1], tm), tok),
                  pl.BlockSpec((1, 3, d), lambda b, i: (b, 0, 0)),
                  pl.BlockSpec(onw.shape, const),
                  pl.BlockSpec(w_out_t.shape, const)],
        out_specs=pl.BlockSpec((1, tm, d), lambda b, i: (b, i, 0)),
        out_shape=jax.ShapeDtypeStruct((bsz, s, d), x.dtype),
        compiler_params=_params(("parallel", "parallel")),
        name="out_proj",
    )(x, ret_t, att_t, mg_t, mod3, onw, w_out_t)


def _score_bound_ok(qn_w, qr_w, kn_w, kr_w):
    def sq(w, d):
        return d * jnp.max(jnp.square(w.astype(_F32)))
    bound2 = (sq(qn_w, MLA_NOPE) + sq(qr_w, MLA_ROPE)) * (sq(kn_w, MLA_NOPE) + sq(kr_w, MLA_ROPE))
    return (bound2 <= SCORE_BOUND_LOG2 ** 2).astype(jnp.int32).reshape(1)


def _col(v):
    return v.astype(_F32).reshape(-1, 1)


def _block_sizes(s):
    tm = min(512, s)
    tm_out = min(1024, s)
    tq = min(512, s)
    tk = min(1024, s // ATTN_BUFS)
    chunk = min(256, s)
    return tm, tm_out, tq, tk, chunk


def kernel(x, c, positions, norm_w, w_ada, b_ada, w_in, ret_decay_logit_fwd, ret_decay_logit_bwd, ret_gn_w, q_norm_w, w_uq, kv_norm_w, w_ukv, qn_nope_w, qn_rope_w, kn_nope_w, kn_rope_w, mla_out_norm_w, w_out):
    bsz, s, d = x.shape
    tm, tm_out, tq, tk, chunk = _block_sizes(s)

    w_in_t = w_in.T.astype(_BF16)
    wq_t = w_uq.T.astype(_BF16)
    wkv_t = w_ukv.T.astype(_BF16)
    w_out_t = w_out.T.astype(_BF16)
    inv64 = (ROPE_BASE ** (-jnp.arange(0, RET_QK, 2, dtype=_F32) / RET_QK)).reshape(-1, 1)
    inv32 = (ROPE_BASE ** (-jnp.arange(0, MLA_ROPE, 2, dtype=_F32) / MLA_ROPE)).reshape(-1, 1)

    mod3 = _ada(c, w_ada, b_ada).reshape(bsz, 3, d)
    pos3 = positions.reshape(bsz, 1, s)

    q_mul = (MLA_NOPE + MLA_ROPE) ** -0.5 * LOG2E
    qn_w = qn_nope_w.astype(_F32) * q_mul
    qr_w = qn_rope_w.astype(_F32) * q_mul

    (rq_t, rk_t, rv_t, rg_t, mg_t, qt, k, vt) = _inproj(
        x, pos3, mod3, norm_w, w_in_t, inv64, inv32,
        _col(q_norm_w), _col(kv_norm_w), _col(kn_rope_w), wq_t, wkv_t,
        _col(qn_w), _col(qr_w), _col(kn_nope_w), tm)

    ret_t = _retention(ret_decay_logit_fwd.astype(_F32), ret_decay_logit_bwd.astype(_F32),
                       rq_t, rk_t, rv_t, rg_t,
                       ret_gn_w.astype(_F32).reshape(RET_HEADS, RET_V, 1), chunk)

    att_t = _attention(_score_bound_ok(qn_w, qr_w, kn_nope_w, kn_rope_w), qt, k, vt, tq, tk)

    return _out_proj(x, ret_t, att_t, mg_t, mod3, _col(mla_out_norm_w), w_out_t, tm_out)
```

```python
import functools

import jax
import jax.numpy as jnp
from jax import lax
from jax.experimental import pallas as pl
from jax.experimental.pallas import tpu as pltpu

RET_HEADS = 8
RET_QK = 64
RET_V = 128
MLA_HEADS = 8
MLA_NOPE = 64
MLA_ROPE = 32
MLA_V = 128
Q_LORA = 384
KV_LORA = 256
ROPE_BASE = 10000.0
EPS = 1e-6
HEAD_PAD = 128
LOG2E = 1.4426950408889634

_SIZES = (RET_HEADS * RET_QK, RET_HEADS * RET_QK, RET_HEADS * RET_V, RET_HEADS * RET_V,
          Q_LORA, KV_LORA, MLA_ROPE, MLA_HEADS * MLA_V)
_OFFS = tuple(sum(_SIZES[:i]) for i in range(len(_SIZES) + 1))

VMEM_LIMIT = 56 * 1024 * 1024

_NT = (((1,), (1,)), ((), ()))
_TN = (((0,), (0,)), ((), ()))
_F32 = jnp.float32
_BF16 = jnp.bfloat16


def _silu(v):
    u = 0.5 * v
    return u + u * jnp.tanh(u)


def _params(sem):
    return pltpu.CompilerParams(dimension_semantics=sem, vmem_limit_bytes=VMEM_LIMIT)


def _ada_kernel(c_ref, w_ref, b_ref, o_ref):
    a = _silu(c_ref[...]).astype(_BF16)
    o_ref[...] = jnp.dot(a, w_ref[...].astype(_BF16), preferred_element_type=_F32) + b_ref[...]


def _ada(c, w_ada, b_ada):
    bsz, d = c.shape
    n = w_ada.shape[1]
    tn = 1024
    return pl.pallas_call(
        _ada_kernel,
        grid=(n // tn,),
        in_specs=[pl.BlockSpec((bsz, d), lambda j: (0, 0)),
                  pl.BlockSpec((d, tn), lambda j: (0, j)),
                  pl.BlockSpec((1, tn), lambda j: (0, j))],
        out_specs=pl.BlockSpec((bsz, tn), lambda j: (0, j)),
        out_shape=jax.ShapeDtypeStruct((bsz, n), _F32),
        compiler_params=_params(("arbitrary",)),
        name="ada_mod",
    )(c, w_ada, b_ada.reshape(1, n))


def _inproj_kernel(x_ref, pos_ref, mod_ref, nw_ref, w_ref, inv64_ref, inv32_ref,
                   qnw_ref, kvnw_ref, krw_ref, wq_ref, wkv_ref, qnn_ref, qrn_ref, knn_ref,
                   rq_ref, rk_ref, rv_ref, rg_ref, mg_ref,
                   qt_ref, k_ref, vt_ref):
    tm = x_ref.shape[1]
    x = x_ref[0]
    shift = mod_ref[0, 0:1, :]
    scale = mod_ref[0, 1:2, :]
    gain = nw_ref[...] * (1.0 + scale)
    h = (x * lax.rsqrt(jnp.mean(x * x, axis=-1, keepdims=True) + EPS) * gain + shift).astype(_BF16)

    def proj(i, j=None):
        return lax.dot_general(w_ref[_OFFS[i]:_OFFS[i + 1 if j is None else j], :], h, _NT,
                               preferred_element_type=_F32)

    def rms(v, w_col):
        return v * lax.rsqrt(jnp.mean(v * v, axis=0, keepdims=True) + EPS) * w_col

    def rot(v, cos, sin):
        n = v.shape[0] // 2
        return v[:n] * cos - v[n:] * sin, v[n:] * cos + v[:n] * sin

    pos = pos_ref[0].astype(_F32)
    ang64 = inv64_ref[...] * pos
    cos64, sin64 = jnp.cos(ang64), jnp.sin(ang64)
    ang32 = inv32_ref[...] * pos
    cos32, sin32 = jnp.cos(ang32), jnp.sin(ang32)

    lat = proj(4, 7)
    cq = rms(lat[:Q_LORA], qnw_ref[...]).astype(_BF16)
    ckv = rms(lat[Q_LORA:Q_LORA + KV_LORA], kvnw_ref[...]).astype(_BF16)
    kr = jnp.concatenate(rot(rms(lat[Q_LORA + KV_LORA:], krw_ref[...]), cos32, sin32), axis=0)
    q_all = jnp.dot(wq_ref[...], cq, preferred_element_type=_F32)
    kv_all = jnp.dot(wkv_ref[...], ckv, preferred_element_type=_F32)
    zpad = jnp.zeros((HEAD_PAD - MLA_NOPE - MLA_ROPE, tm), _F32)
    hq = MLA_NOPE + MLA_ROPE
    hkv = MLA_NOPE + MLA_V
    for hd in range(MLA_HEADS):
        qn = rms(q_all[hd * hq: hd * hq + MLA_NOPE], qnn_ref[...])
        qr = rot(rms(q_all[hd * hq + MLA_NOPE: (hd + 1) * hq], qrn_ref[...]), cos32, sin32)
        qt_ref[0, hd] = jnp.concatenate([qn, qr[0], qr[1], zpad], axis=0).astype(_BF16)
        kn = rms(kv_all[hd * hkv: hd * hkv + MLA_NOPE], knn_ref[...])
        kh = jnp.concatenate([kn, kr, zpad], axis=0)
        k_ref[0, hd] = kh.T.astype(_BF16)
        vt_ref[0, hd] = kv_all[hd * hkv + MLA_NOPE: (hd + 1) * hkv].astype(_BF16)

    rg_ref[0] = _silu(proj(3)).astype(_BF16)
    mg_ref[0] = _silu(proj(7)).astype(_BF16)
    qk = proj(0, 2)
    k_mul = RET_QK ** -0.5
    for src, dst, cs, sn in ((0, rq_ref, cos64, sin64), (1, rk_ref, cos64 * k_mul, sin64 * k_mul)):
        for hd in range(RET_HEADS):
            lo = _OFFS[src] + hd * RET_QK
            r1, r2 = rot(qk[lo: lo + RET_QK], cs, sn)
            dst[0, hd * RET_QK: hd * RET_QK + RET_QK // 2, :] = r1.astype(_BF16)
            dst[0, hd * RET_QK + RET_QK // 2: (hd + 1) * RET_QK, :] = r2.astype(_BF16)
    rv_ref[0] = proj(2).astype(_BF16)


def _inproj(x, pos3, mod3, norm_w, w_in_t, inv64, inv32, qnw, kvnw, krw, wq_t, wkv_t, qnn, qrn, knn, tm):
    bsz, s, d = x.shape
    const = lambda b, i: (0, 0)
    tok = lambda b, i: (b, 0, i)
    head_t = lambda b, i: (b, 0, 0, i)
    sizes = (_SIZES[0], _SIZES[1], _SIZES[2], _SIZES[3], _SIZES[7])
    out_shape = [jax.ShapeDtypeStruct((bsz, n, s), _BF16) for n in sizes]
    out_specs = [pl.BlockSpec((1, n, tm), tok) for n in sizes]
    out_shape += [jax.ShapeDtypeStruct((bsz, MLA_HEADS, HEAD_PAD, s), _BF16),
                  jax.ShapeDtypeStruct((bsz, MLA_HEADS, s, HEAD_PAD), _BF16),
                  jax.ShapeDtypeStruct((bsz, MLA_HEADS, MLA_V, s), _BF16)]
    out_specs += [pl.BlockSpec((1, MLA_HEADS, HEAD_PAD, tm), head_t),
                  pl.BlockSpec((1, MLA_HEADS, tm, HEAD_PAD), lambda b, i: (b, 0, i, 0)),
                  pl.BlockSpec((1, MLA_HEADS, MLA_V, tm), head_t)]
    consts = (w_in_t, inv64, inv32, qnw, kvnw, krw, wq_t, wkv_t, qnn, qrn, knn)
    return pl.pallas_call(
        _inproj_kernel,
        grid=(bsz, s // tm),
        in_specs=[pl.BlockSpec((1, tm, d), lambda b, i: (b, i, 0)),
                  pl.BlockSpec((1, 1, tm), tok),
                  pl.BlockSpec((1, 3, d), lambda b, i: (b, 0, 0)),
                  pl.BlockSpec((1, d), const)] + [pl.BlockSpec(a.shape, const) for a in consts],
        out_specs=out_specs,
        out_shape=out_shape,
        compiler_params=_params(("parallel", "parallel")),
        name="in_proj",
    )(x, pos3, mod3, norm_w.reshape(1, d), *consts)


RET_UNROLL = 32


def _log_sigmoid(v):
    return jnp.minimum(v, 0.0) - jnp.log(1.0 + jnp.exp(-jnp.abs(v)))


def _retention_kernel(lf_ref, lb_ref, q_ref, k_ref, v_ref, g_ref, gnw_ref, o_ref,
                      kv_ref, st_ref, *, chunk, n_chunks):
    hd = pl.program_id(1)
    c = chunk
    lgf = _log_sigmoid(jnp.full((1, 1), lf_ref[hd], _F32))
    lgb = _log_sigmoid(jnp.full((1, 1), lb_ref[hd], _F32))

    pos_l = lax.broadcasted_iota(jnp.int32, (1, c), 1).astype(_F32)
    kdec_f = jnp.exp(lgf * (c - 1.0 - pos_l))
    kdec_b = jnp.exp(lgb * pos_l)
    qdec_f = jnp.exp(lgf * (pos_l + 1.0))
    qdec_b = jnp.exp(lgb * (c - pos_l))
    jj = lax.broadcasted_iota(jnp.int32, (c, c), 0).astype(_F32)
    ii = lax.broadcasted_iota(jnp.int32, (c, c), 1).astype(_F32)
    dmat = jnp.where(ii >= jj, jnp.exp(lgf * jnp.maximum(ii - jj, 0.0)),
                     jnp.exp(lgb * jnp.maximum(jj - ii, 0.0)))
    lane = lax.broadcasted_iota(jnp.int32, (1, 2 * RET_QK), 1)
    is_f = lane < RET_QK
    sdec = jnp.where(is_f, jnp.exp(lgf * c), jnp.exp(lgb * c))

    def chunk_ds(i):
        return pl.ds(pl.multiple_of(i * c, c), c)

    def kv_body(i, carry):
        ds = chunk_ds(i)
        kc = k_ref[0, :, ds].astype(_F32)
        kfb = jnp.concatenate([kc * kdec_f, kc * kdec_b], axis=0).astype(_BF16)
        kv_ref[i] = lax.dot_general(v_ref[0, :, ds], kfb, _NT, preferred_element_type=_F32)
        return carry

    unroll = min(n_chunks, RET_UNROLL)
    lax.fori_loop(0, n_chunks, kv_body, 0, unroll=unroll)

    def bwd_body(t, sb):
        i = n_chunks - 1 - t
        st_ref[i] = sb
        return sb * sdec + kv_ref[i]

    lax.fori_loop(0, n_chunks, bwd_body, jnp.zeros((RET_V, 2 * RET_QK), _F32))

    gnw = gnw_ref[0]

    def out_body(i, sf):
        ds = chunk_ds(i)
        qc = q_ref[0, :, ds]
        kc = k_ref[0, :, ds]
        st = lax.dot_general(kc, qc, _TN, preferred_element_type=_F32)
        pt = (st * dmat).astype(_BF16)
        qf = qc.astype(_F32)
        qfb = jnp.concatenate([qf * qdec_f, qf * qdec_b], axis=0).astype(_BF16)
        state = jnp.where(is_f, sf, st_ref[i]).astype(_BF16)
        o = (jnp.dot(v_ref[0, :, ds], pt, preferred_element_type=_F32)
             + jnp.dot(state, qfb, preferred_element_type=_F32))
        mu = jnp.mean(o, axis=0, keepdims=True)
        dlt = o - mu
        var = jnp.mean(dlt * dlt, axis=0, keepdims=True)
        on = dlt * lax.rsqrt(var + EPS) * gnw
        o_ref[0, :, ds] = on.astype(_BF16) * g_ref[0, :, ds]
        return sf * sdec + kv_ref[i]

    lax.fori_loop(0, n_chunks, out_body, jnp.zeros((RET_V, 2 * RET_QK), _F32), unroll=unroll)


def _retention(lf, lb, rq_t, rk_t, rv_t, rg_t, gnw3, chunk):
    bsz, _, s = rq_t.shape
    n_chunks = s // chunk
    smem = pl.BlockSpec(memory_space=pltpu.SMEM)
    return pl.pallas_call(
        functools.partial(_retention_kernel, chunk=chunk, n_chunks=n_chunks),
        grid=(bsz, RET_HEADS),
        in_specs=[smem, smem,
                  pl.BlockSpec((1, RET_QK, s), lambda b, h: (b, h, 0)),
                  pl.BlockSpec((1, RET_QK, s), lambda b, h: (b, h, 0)),
                  pl.BlockSpec((1, RET_V, s), lambda b, h: (b, h, 0)),
                  pl.BlockSpec((1, RET_V, s), lambda b, h: (b, h, 0)),
                  pl.BlockSpec((1, RET_V, 1), lambda b, h: (h, 0, 0))],
        out_specs=pl.BlockSpec((1, RET_V, s), lambda b, h: (b, h, 0)),
        out_shape=jax.ShapeDtypeStruct((bsz, RET_HEADS * RET_V, s), _BF16),
        scratch_shapes=[pltpu.VMEM((n_chunks, RET_V, 2 * RET_QK), _F32),
                        pltpu.VMEM((n_chunks, RET_V, 2 * RET_QK), _F32)],
        compiler_params=_params(("parallel", "parallel")),
        name="retention",
    )(lf, lb, rq_t, rk_t, rv_t, rg_t, gnw3)


def _attn_kernel(fast_ref, qt_ref, k_ref, vt_ref, o_ref, *bufs, tq, tk, tqf, tkf, n_q, n_kv):
    nb = len(bufs)
    fast = fast_ref[0]

    def q_ds(qi):
        return pl.ds(pl.multiple_of(qi * tq, tq), tq)

    def kv_ds(j):
        return pl.ds(pl.multiple_of(j * tk, tk), tk)

    def write(qi, l, acc):
        o_ref[0, :, q_ds(qi)] = (acc * (1.0 / l)).astype(_BF16)

    @pl.when(fast != 0)
    def _fast_path():
        def body(qi, carry):
            qds = pl.ds(pl.multiple_of(qi * tqf, tqf), tqf)
            qt = qt_ref[0, 0, :, qds]
            l = jnp.zeros((1, tqf), _F32)
            acc = jnp.zeros((MLA_V, tqf), _F32)
            for j in range(n_kv * tk // tkf):
                ds = pl.ds(j * tkf, tkf)
                st = jnp.dot(k_ref[0, 0, ds, :], qt, preferred_element_type=_F32)
                p = jnp.exp2(st)
                l = l + jnp.sum(p, axis=0, keepdims=True)
                acc = acc + jnp.dot(vt_ref[0, 0, :, ds], p.astype(_BF16),
                                    preferred_element_type=_F32)
            o_ref[0, :, qds] = (acc * (1.0 / l)).astype(_BF16)
            return carry

        lax.fori_loop(0, n_q * tq // tqf, body, 0)

    @pl.when(fast == 0)
    def _safe_path():
        def scores(qi, j, m, s_ref):
            st = jnp.dot(k_ref[0, 0, kv_ds(j), :], qt_ref[0, 0, :, q_ds(qi)],
                         preferred_element_type=_F32)
            s_ref[...] = st
            return jnp.maximum(m, jnp.max(st, axis=0, keepdims=True))

        def accumulate(j, m_prev, m, l, acc, s_ref):
            alpha = jnp.exp2(m_prev - m)
            p = jnp.exp2(s_ref[...] - m)
            l = alpha * l + jnp.sum(p, axis=0, keepdims=True)
            acc = alpha * acc + jnp.dot(vt_ref[0, 0, :, kv_ds(j)], p.astype(_BF16),
                                        preferred_element_type=_F32)
            return l, acc

        neg_inf = jnp.full((1, tq), -jnp.inf, _F32)

        def body(qi, carry):
            ms = [neg_inf] + list(carry)
            nxt = []
            qn = jnp.minimum(qi + 1, n_q - 1)
            l = jnp.zeros((1, tq), _F32)
            acc = jnp.zeros((MLA_V, tq), _F32)
            for j in range(n_kv):
                t = j + ATTN_AHEAD
                if t < n_kv:
                    ms.append(scores(qi, t, ms[-1], bufs[t % nb]))
                else:
                    nxt.append(scores(qn, t - n_kv, nxt[-1] if nxt else neg_inf, bufs[t % nb]))
                l, acc = accumulate(j, ms[j], ms[j + 1], l, acc, bufs[j % nb])
            write(qi, l, acc)
            return tuple(nxt)

        first = []
        for j in range(ATTN_AHEAD):
            first.append(scores(0, j, first[-1] if first else neg_inf, bufs[j]))
        lax.fori_loop(0, n_q, body, tuple(first))


ATTN_AHEAD = 2
ATTN_BUFS = 4
SCORE_BOUND_LOG2 = 60.0
ATTN_FAST_TK = 4096
ATTN_FAST_TQ = 1024


def _attention(fast, qt, k, vt, tq, tk):
    bsz, nh, _, s = qt.shape
    n_kv = s // tk
    assert s % tq == 0 and s % tk == 0 and n_kv % ATTN_BUFS == 0 and ATTN_AHEAD < ATTN_BUFS
    return pl.pallas_call(
        functools.partial(_attn_kernel, tq=tq, tk=tk, tqf=min(ATTN_FAST_TQ, s), tkf=min(ATTN_FAST_TK, s),
                          n_q=s // tq, n_kv=s // tk),
        grid=(bsz, nh),
        in_specs=[pl.BlockSpec(memory_space=pltpu.SMEM),
                  pl.BlockSpec((1, 1, HEAD_PAD, s), lambda b, h: (b, h, 0, 0)),
                  pl.BlockSpec((1, 1, s, HEAD_PAD), lambda b, h: (b, h, 0, 0)),
                  pl.BlockSpec((1, 1, MLA_V, s), lambda b, h: (b, h, 0, 0))],
        out_specs=pl.BlockSpec((1, MLA_V, s), lambda b, h: (b, h, 0)),
        out_shape=jax.ShapeDtypeStruct((bsz, nh * MLA_V, s), _BF16),
        scratch_shapes=[pltpu.VMEM((tk, tq), _F32)] * ATTN_BUFS,
        compiler_params=_params(("parallel", "parallel")),
        name="mla_attention",
    )(fast, qt, k, vt)


OUT_BLOCK = 256


def _out_kernel(x_ref, ret_ref, att_ref, mg_ref, mod_ref, onw_ref, w_ref, o_ref):
    att = att_ref[0].astype(_F32)
    mla = att * lax.rsqrt(jnp.mean(att * att, axis=0, keepdims=True) + EPS) * onw_ref[...]
    mla = (mla * mg_ref[0].astype(_F32)).astype(_BF16)
    cat = jnp.concatenate([ret_ref[0], mla], axis=0)
    d = w_ref.shape[0]
    for lo in range(0, d, OUT_BLOCK):
        yt = jnp.dot(w_ref[lo:lo + OUT_BLOCK, :], cat, preferred_element_type=_F32)
        gate = mod_ref[0, 2:3, lo:lo + OUT_BLOCK]
        o_ref[0, :, lo:lo + OUT_BLOCK] = x_ref[0, :, lo:lo + OUT_BLOCK] + gate * yt.T


def _out_proj(x, ret_t, att_t, mg_t, mod3, onw, w_out_t, tm):
    bsz, s, d = x.shape
    tok = lambda b, i: (b, 0, i)
    const = lambda b, i: (0, 0)
    return pl.pallas_call(
        _out_kernel,
        grid=(bsz, s // tm),
        in_specs=[pl.BlockSpec((1, tm, d), lambda b, i: (b, i, 0)),
                  pl.BlockSpec((1, ret_t.shape[1], tm), tok),
                  pl.BlockSpec((1, att_t.shape[1], tm), tok),
                  pl.BlockSpec((1, mg_t.shape[1], tm), tok),
                  pl.BlockSpec((1, 3, d), lambda b, i: (b, 0, 0)),
                  pl.BlockSpec(onw.shape, const),
                  pl.BlockSpec(w_out_t.shape, const)],
        out_specs=pl.BlockSpec((1, tm, d), lambda b, i: (b, i, 0)),
        out_shape=jax.ShapeDtypeStruct((bsz, s, d), x.dtype),
        compiler_params=_params(("parallel", "parallel")),
        name="out_proj",
    )(x, ret_t, att_t, mg_t, mod3, onw, w_out_t)


def _score_bound_ok(qn_w, qr_w, kn_w, kr_w):
    def sq(w, d):
        return d * jnp.max(jnp.square(w.astype(_F32)))
    bound2 = (sq(qn_w, MLA_NOPE) + sq(qr_w, MLA_ROPE)) * (sq(kn_w, MLA_NOPE) + sq(kr_w, MLA_ROPE))
    return (bound2 <= SCORE_BOUND_LOG2 ** 2).astype(jnp.int32).reshape(1)


def _col(v):
    return v.astype(_F32).reshape(-1, 1)


def _block_sizes(s):
    tm = min(512, s)
    tm_out = min(1024, s)
    tq = min(512, s)
    tk = min(1024, s // ATTN_BUFS)
    chunk = min(256, s)
    return tm, tm_out, tq, tk, chunk


def kernel(x, c, positions, norm_w, w_ada, b_ada, w_in, ret_decay_logit_fwd, ret_decay_logit_bwd, ret_gn_w, q_norm_w, w_uq, kv_norm_w, w_ukv, qn_nope_w, qn_rope_w, kn_nope_w, kn_rope_w, mla_out_norm_w, w_out):
    bsz, s, d = x.shape
    tm, tm_out, tq, tk, chunk = _block_sizes(s)
    assert s % tm == 0 and s % tm_out == 0 and s % chunk == 0 and d == w_in.shape[0]

    w_in_t = w_in.T.astype(_BF16)
    wq_t = w_uq.T.astype(_BF16)
    wkv_t = w_ukv.T.astype(_BF16)
    w_out_t = w_out.T.astype(_BF16)
    inv64 = (ROPE_BASE ** (-jnp.arange(0, RET_QK, 2, dtype=_F32) / RET_QK)).reshape(-1, 1)
    inv32 = (ROPE_BASE ** (-jnp.arange(0, MLA_ROPE, 2, dtype=_F32) / MLA_ROPE)).reshape(-1, 1)

    mod3 = _ada(c, w_ada, b_ada).reshape(bsz, 3, d)
    pos3 = positions.reshape(bsz, 1, s)

    q_mul = (MLA_NOPE + MLA_ROPE) ** -0.5 * LOG2E
    qn_w = qn_nope_w.astype(_F32) * q_mul
    qr_w = qn_rope_w.astype(_F32) * q_mul

    (rq_t, rk_t, rv_t, rg_t, mg_t, qt, k, vt) = _inproj(
        x, pos3, mod3, norm_w, w_in_t, inv64, inv32,
        _col(q_norm_w), _col(kv_norm_w), _col(kn_rope_w), wq_t, wkv_t,
        _col(qn_w), _col(qr_w), _col(kn_nope_w), tm)

    ret_t = _retention(ret_decay_logit_fwd.astype(_F32), ret_decay_logit_bwd.astype(_F32),
                       rq_t, rk_t, rv_t, rg_t,
                       ret_gn_w.astype(_F32).reshape(RET_HEADS, RET_V, 1), chunk)

    att_t = _attention(_score_bound_ok(qn_w, qr_w, kn_nope_w, kn_rope_w), qt, k, vt, tq, tk)

    return _out_proj(x, ret_t, att_t, mg_t, mod3, _col(mla_out_norm_w), w_out_t, tm_out)
```

```python
import functools

import jax
import jax.numpy as jnp
from jax import lax
from jax.experimental import pallas as pl
from jax.experimental.pallas import tpu as pltpu

RET_HEADS = 8
RET_QK = 64
RET_V = 128
MLA_HEADS = 8
MLA_NOPE = 64
MLA_ROPE = 32
MLA_V = 128
Q_LORA = 384
KV_LORA = 256
ROPE_BASE = 10000.0
EPS = 1e-6
HEAD_PAD = 128
LOG2E = 1.4426950408889634

_SIZES = (RET_HEADS * RET_QK, RET_HEADS * RET_QK, RET_HEADS * RET_V, RET_HEADS * RET_V,
          Q_LORA, KV_LORA, MLA_ROPE, MLA_HEADS * MLA_V)
_OFFS = tuple(sum(_SIZES[:i]) for i in range(len(_SIZES) + 1))

VMEM_LIMIT = 56 * 1024 * 1024

_NT = (((1,), (1,)), ((), ()))
_TN = (((0,), (0,)), ((), ()))
_F32 = jnp.float32
_BF16 = jnp.bfloat16


def _silu(v):
    u = 0.5 * v
    return u + u * jnp.tanh(u)


def _params(sem):
    return pltpu.CompilerParams(dimension_semantics=sem, vmem_limit_bytes=VMEM_LIMIT)


def _ada_kernel(c_ref, w_ref, b_ref, o_ref):
    a = _silu(c_ref[...]).astype(_BF16)
    o_ref[...] = jnp.dot(a, w_ref[...].astype(_BF16), preferred_element_type=_F32) + b_ref[...]


def _ada(c, w_ada, b_ada):
    bsz, d = c.shape
    n = w_ada.shape[1]
    tn = 1024
    return pl.pallas_call(
        _ada_kernel,
        grid=(n // tn,),
        in_specs=[pl.BlockSpec((bsz, d), lambda j: (0, 0)),
                  pl.BlockSpec((d, tn), lambda j: (0, j)),
                  pl.BlockSpec((1, tn), lambda j: (0, j))],
        out_specs=pl.BlockSpec((bsz, tn), lambda j: (0, j)),
        out_shape=jax.ShapeDtypeStruct((bsz, n), _F32),
        compiler_params=_params(("arbitrary",)),
        name="ada_mod",
    )(c, w_ada, b_ada.reshape(1, n))


def _inproj_kernel(x_ref, pos_ref, mod_ref, nw_ref, w_ref, inv64_ref, inv32_ref,
                   qnw_ref, kvnw_ref, krw_ref, wq_ref, wkv_ref, qnn_ref, qrn_ref, knn_ref,
                   rq_ref, rk_ref, rv_ref, rg_ref, mg_ref,
                   qt_ref, k_ref, vt_ref):
    tm = x_ref.shape[1]
    x = x_ref[0]
    shift = mod_ref[0, 0:1, :]
    scale = mod_ref[0, 1:2, :]
    gain = nw_ref[...] * (1.0 + scale)
    h = (x * lax.rsqrt(jnp.mean(x * x, axis=-1, keepdims=True) + EPS) * gain + shift).astype(_BF16)

    def proj(i, j=None):
        return lax.dot_general(w_ref[_OFFS[i]:_OFFS[i + 1 if j is None else j], :], h, _NT,
                               preferred_element_type=_F32)

    def rms(v, w_col):
        return v * lax.rsqrt(jnp.mean(v * v, axis=0, keepdims=True) + EPS) * w_col

    def rot(v, cos, sin):
        n = v.shape[0] // 2
        return v[:n] * cos - v[n:] * sin, v[n:] * cos + v[:n] * sin

    pos = pos_ref[0].astype(_F32)
    ang64 = inv64_ref[...] * pos
    cos64, sin64 = jnp.cos(ang64), jnp.sin(ang64)
    ang32 = inv32_ref[...] * pos
    cos32, sin32 = jnp.cos(ang32), jnp.sin(ang32)

    lat = proj(4, 7)
    cq = rms(lat[:Q_LORA], qnw_ref[...]).astype(_BF16)
    ckv = rms(lat[Q_LORA:Q_LORA + KV_LORA], kvnw_ref[...]).astype(_BF16)
    kr = jnp.concatenate(rot(rms(lat[Q_LORA + KV_LORA:], krw_ref[...]), cos32, sin32), axis=0)
    q_all = jnp.dot(wq_ref[...], cq, preferred_element_type=_F32)
    kv_all = jnp.dot(wkv_ref[...], ckv, preferred_element_type=_F32)
    zpad = jnp.zeros((HEAD_PAD - MLA_NOPE - MLA_ROPE, tm), _F32)
    hq = MLA_NOPE + MLA_ROPE
    hkv = MLA_NOPE + MLA_V
    for hd in range(MLA_HEADS):
        qn = rms(q_all[hd * hq: hd * hq + MLA_NOPE], qnn_ref[...])
        qr = rot(rms(q_all[hd * hq + MLA_NOPE: (hd + 1) * hq], qrn_ref[...]), cos32, sin32)
        qt_ref[0, hd] = jnp.concatenate([qn, qr[0], qr[1], zpad], axis=0).astype(_BF16)
        kn = rms(kv_all[hd * hkv: hd * hkv + MLA_NOPE], knn_ref[...])
        kh = jnp.concatenate([kn, kr, zpad], axis=0)
        k_ref[0, hd] = kh.T.astype(_BF16)
        vt_ref[0, hd] = kv_all[hd * hkv + MLA_NOPE: (hd + 1) * hkv].astype(_BF16)

    rg_ref[0] = _silu(proj(3)).astype(_BF16)
    mg_ref[0] = _silu(proj(7)).astype(_BF16)
    qk = proj(0, 2)
    k_mul = RET_QK ** -0.5
    for src, dst, cs, sn in ((0, rq_ref, cos64, sin64), (1, rk_ref, cos64 * k_mul, sin64 * k_mul)):
        for hd in range(RET_HEADS):
            lo = _OFFS[src] + hd * RET_QK
            r1, r2 = rot(qk[lo: lo + RET_QK], cs, sn)
            dst[0, hd * RET_QK: hd * RET_QK + RET_QK // 2, :] = r1.astype(_BF16)
            dst[0, hd * RET_QK + RET_QK // 2: (hd + 1) * RET_QK, :] = r2.astype(_BF16)
    rv_ref[0] = proj(2).astype(_BF16)


def _inproj(x, pos3, mod3, norm_w, w_in_t, inv64, inv32, qnw, kvnw, krw, wq_t, wkv_t, qnn, qrn, knn, tm):
    bsz, s, d = x.shape
    const = lambda b, i: (0, 0)
    tok = lambda b, i: (b, 0, i)
    head_t = lambda b, i: (b, 0, 0, i)
    sizes = (_SIZES[0], _SIZES[1], _SIZES[2], _SIZES[3], _SIZES[7])
    out_shape = [jax.ShapeDtypeStruct((bsz, n, s), _BF16) for n in sizes]
    out_specs = [pl.BlockSpec((1, n, tm), tok) for n in sizes]
    out_shape += [jax.ShapeDtypeStruct((bsz, MLA_HEADS, HEAD_PAD, s), _BF16),
                  jax.ShapeDtypeStruct((bsz, MLA_HEADS, s, HEAD_PAD), _BF16),
                  jax.ShapeDtypeStruct((bsz, MLA_HEADS, MLA_V, s), _BF16)]
    out_specs += [pl.BlockSpec((1, MLA_HEADS, HEAD_PAD, tm), head_t),
                  pl.BlockSpec((1, MLA_HEADS, tm, HEAD_PAD), lambda b, i: (b, 0, i, 0)),
                  pl.BlockSpec((1, MLA_HEADS, MLA_V, tm), head_t)]
    consts = (w_in_t, inv64, inv32, qnw, kvnw, krw, wq_t, wkv_t, qnn, qrn, knn)
    return pl.pallas_call(
        _inproj_kernel,
        grid=(bsz, s // tm),
        in_specs=[pl.BlockSpec((1, tm, d), lambda b, i: (b, i, 0)),
                  pl.BlockSpec((1, 1, tm), tok),
                  pl.BlockSpec((1, 3, d), lambda b, i: (b, 0, 0)),
                  pl.BlockSpec((1, d), const)] + [pl.BlockSpec(a.shape, const) for a in consts],
        out_specs=out_specs,
        out_shape=out_shape,
        compiler_params=_params(("parallel", "parallel")),
        name="in_proj",
    )(x, pos3, mod3, norm_w.reshape(1, d), *consts)


RET_UNROLL = 32


def _log_sigmoid(v):
    return jnp.minimum(v, 0.0) - jnp.log(1.0 + jnp.exp(-jnp.abs(v)))


def _retention_kernel(lf_ref, lb_ref, q_ref, k_ref, v_ref, g_ref, gnw_ref, o_ref,
                      kv_ref, st_ref, *, chunk, n_chunks):
    hd = pl.program_id(1)
    c = chunk
    lgf = _log_sigmoid(jnp.full((1, 1), lf_ref[hd], _F32))
    lgb = _log_sigmoid(jnp.full((1, 1), lb_ref[hd], _F32))

    pos_l = lax.broadcasted_iota(jnp.int32, (1, c), 1).astype(_F32)
    kdec_f = jnp.exp(lgf * (c - 1.0 - pos_l))
    kdec_b = jnp.exp(lgb * pos_l)
    qdec_f = jnp.exp(lgf * (pos_l + 1.0))
    qdec_b = jnp.exp(lgb * (c - pos_l))
    jj = lax.broadcasted_iota(jnp.int32, (c, c), 0).astype(_F32)
    ii = lax.broadcasted_iota(jnp.int32, (c, c), 1).astype(_F32)
    dmat = jnp.where(ii >= jj, jnp.exp(lgf * jnp.maximum(ii - jj, 0.0)),
                     jnp.exp(lgb * jnp.maximum(jj - ii, 0.0)))
    lane = lax.broadcasted_iota(jnp.int32, (1, 2 * RET_QK), 1)
    is_f = lane < RET_QK
    sdec = jnp.where(is_f, jnp.exp(lgf * c), jnp.exp(lgb * c))

    def chunk_ds(i):
        return pl.ds(pl.multiple_of(i * c, c), c)

    def kv_body(i, carry):
        ds = chunk_ds(i)
        kc = k_ref[0, :, ds].astype(_F32)
        kfb = jnp.concatenate([kc * kdec_f, kc * kdec_b], axis=0).astype(_BF16)
        kv_ref[i] = lax.dot_general(v_ref[0, :, ds], kfb, _NT, preferred_element_type=_F32)
        return carry

    unroll = min(n_chunks, RET_UNROLL)
    lax.fori_loop(0, n_chunks, kv_body, 0, unroll=unroll)

    def bwd_body(t, sb):
        i = n_chunks - 1 - t
        st_ref[i] = sb
        return sb * sdec + kv_ref[i]

    lax.fori_loop(0, n_chunks, bwd_body, jnp.zeros((RET_V, 2 * RET_QK), _F32))

    gnw = gnw_ref[0]

    def out_body(i, sf):
        ds = chunk_ds(i)
        qc = q_ref[0, :, ds]
        kc = k_ref[0, :, ds]
        st = lax.dot_general(kc, qc, _TN, preferred_element_type=_F32)
        pt = (st * dmat).astype(_BF16)
        qf = qc.astype(_F32)
        qfb = jnp.concatenate([qf * qdec_f, qf * qdec_b], axis=0).astype(_BF16)
        state = jnp.where(is_f, sf, st_ref[i]).astype(_BF16)
        o = (jnp.dot(v_ref[0, :, ds], pt, preferred_element_type=_F32)
             + jnp.dot(state, qfb, preferred_element_type=_F32))
        mu = jnp.mean(o, axis=0, keepdims=True)
        dlt = o - mu
        var = jnp.mean(dlt * dlt, axis=0, keepdims=True)
        on = dlt * lax.rsqrt(var + EPS) * gnw
        o_ref[0, :, ds] = on.astype(_BF16) * g_ref[0, :, ds]
        return sf * sdec + kv_ref[i]

    lax.fori_loop(0, n_chunks, out_body, jnp.zeros((RET_V, 2 * RET_QK), _F32), unroll=unroll)


def _retention(lf, lb, rq_t, rk_t, rv_t, rg_t, gnw3, chunk):
    bsz, _, s = rq_t.shape
    n_chunks = s // chunk
    smem = pl.BlockSpec(memory_space=pltpu.SMEM)
    return pl.pallas_call(
        functools.partial(_retention_kernel, chunk=chunk, n_chunks=n_chunks),
        grid=(bsz, RET_HEADS),
        in_specs=[smem, smem,
                  pl.BlockSpec((1, RET_QK, s), lambda b, h: (b, h, 0)),
                  pl.BlockSpec((1, RET_QK, s), lambda b, h: (b, h, 0)),
                  pl.BlockSpec((1, RET_V, s), lambda b, h: (b, h, 0)),
                  pl.BlockSpec((1, RET_V, s), lambda b, h: (b, h, 0)),
                  pl.BlockSpec((1, RET_V, 1), lambda b, h: (h, 0, 0))],
        out_specs=pl.BlockSpec((1, RET_V, s), lambda b, h: (b, h, 0)),
        out_shape=jax.ShapeDtypeStruct((bsz, RET_HEADS * RET_V, s), _BF16),
        scratch_shapes=[pltpu.VMEM((n_chunks, RET_V, 2 * RET_QK), _F32),
                        pltpu.VMEM((n_chunks, RET_V, 2 * RET_QK), _F32)],
        compiler_params=_params(("parallel", "parallel")),
        name="retention",
    )(lf, lb, rq_t, rk_t, rv_t, rg_t, gnw3)


def _attn_kernel(fast_ref, qt_ref, k_ref, vt_ref, o_ref, *bufs, tq, tk, tqf, tkf, n_q, n_kv):
    nb = len(bufs)
    fast = fast_ref[0]

    def q_ds(qi):
        return pl.ds(pl.multiple_of(qi * tq, tq), tq)

    def kv_ds(j):
        return pl.ds(pl.multiple_of(j * tk, tk), tk)

    def write(qi, l, acc):
        o_ref[0, :, q_ds(qi)] = (acc * (1.0 / l)).astype(_BF16)

    @pl.when(fast != 0)
    def _fast_path():
        def body(qi, carry):
            qds = pl.ds(pl.multiple_of(qi * tqf, tqf), tqf)
            qt = qt_ref[0, 0, :, qds]
            l = jnp.zeros((1, tqf), _F32)
            acc = jnp.zeros((MLA_V, tqf), _F32)
            for j in range(n_kv * tk // tkf):
                ds = pl.ds(j * tkf, tkf)
                st = jnp.dot(k_ref[0, 0, ds, :], qt, preferred_element_type=_F32)
                p = jnp.exp2(st)
                l = l + jnp.sum(p, axis=0, keepdims=True)
                acc = acc + jnp.dot(vt_ref[0, 0, :, ds], p.astype(_BF16),
                                    preferred_element_type=_F32)
            o_ref[0, :, qds] = (acc * (1.0 / l)).astype(_BF16)
            return carry

        lax.fori_loop(0, n_q * tq // tqf, body, 0)

    @pl.when(fast == 0)
    def _safe_path():
        def scores(qi, j, m, s_ref):
            st = jnp.dot(k_ref[0, 0, kv_ds(j), :], qt_ref[0, 0, :, q_ds(qi)],
                         preferred_element_type=_F32)
            s_ref[...] = st
            return jnp.maximum(m, jnp.max(st, axis=0, keepdims=True))

        def accumulate(j, m_prev, m, l, acc, s_ref):
            alpha = jnp.exp2(m_prev - m)
            p = jnp.exp2(s_ref[...] - m)
            l = alpha * l + jnp.sum(p, axis=0, keepdims=True)
            acc = alpha * acc + jnp.dot(vt_ref[0, 0, :, kv_ds(j)], p.astype(_BF16),
                                        preferred_element_type=_F32)
            return l, acc

        neg_inf = jnp.full((1, tq), -jnp.inf, _F32)

        def body(qi, carry):
            ms = [neg_inf] + list(carry)
            nxt = []
            qn = jnp.minimum(qi + 1, n_q - 1)
            l = jnp.zeros((1, tq), _F32)
            acc = jnp.zeros((MLA_V, tq), _F32)
            for j in range(n_kv):
                t = j + ATTN_AHEAD
                if t < n_kv:
                    ms.append(scores(qi, t, ms[-1], bufs[t % nb]))
                else:
                    nxt.append(scores(qn, t - n_kv, nxt[-1] if nxt else neg_inf, bufs[t % nb]))
                l, acc = accumulate(j, ms[j], ms[j + 1], l, acc, bufs[j % nb])
            write(qi, l, acc)
            return tuple(nxt)

        first = []
        for j in range(ATTN_AHEAD):
            first.append(scores(0, j, first[-1] if first else neg_inf, bufs[j]))
        lax.fori_loop(0, n_q, body, tuple(first))


ATTN_AHEAD = 2
ATTN_BUFS = 4
SCORE_BOUND_LOG2 = 60.0
ATTN_FAST_TK = 4096
ATTN_FAST_TQ = 1024


def _attention(fast, qt, k, vt, tq, tk):
    bsz, nh, _, s = qt.shape
    n_kv = s // tk
    assert s % tq == 0 and s % tk == 0 and n_kv % ATTN_BUFS == 0 and ATTN_AHEAD < ATTN_BUFS
    return pl.pallas_call(
        functools.partial(_attn_kernel, tq=tq, tk=tk, tqf=min(ATTN_FAST_TQ, s), tkf=min(ATTN_FAST_TK, s),
                          n_q=s // tq, n_kv=s // tk),
        grid=(bsz, nh),
        in_specs=[pl.BlockSpec(memory_space=pltpu.SMEM),
                  pl.BlockSpec((1, 1, HEAD_PAD, s), lambda b, h: (b, h, 0, 0)),
                  pl.BlockSpec((1, 1, s, HEAD_PAD), lambda b, h: (b, h, 0, 0)),
                  pl.BlockSpec((1, 1, MLA_V, s), lambda b, h: (b, h, 0, 0))],
        out_specs=pl.BlockSpec((1, MLA_V, s), lambda b, h: (b, h, 0)),
        out_shape=jax.ShapeDtypeStruct((bsz, nh * MLA_V, s), _BF16),
        scratch_shapes=[pltpu.VMEM((tk, tq), _F32)] * ATTN_BUFS,
        compiler_params=_params(("parallel", "parallel")),
        name="mla_attention",
    )(fast, qt, k, vt)


OUT_BLOCK = 512


def _out_kernel(x_ref, ret_ref, att_ref, mg_ref, mod_ref, onw_ref, w_ref, o_ref):
    att = att_ref[0].astype(_F32)
    mla = att * lax.rsqrt(jnp.mean(att * att, axis=0, keepdims=True) + EPS) * onw_ref[...]
    mla = (mla * mg_ref[0].astype(_F32)).astype(_BF16)
    cat = jnp.concatenate([ret_ref[0], mla], axis=0)
    d = w_ref.shape[0]
    for lo in range(0, d, OUT_BLOCK):
        yt = jnp.dot(w_ref[lo:lo + OUT_BLOCK, :], cat, preferred_element_type=_F32)
        gate = mod_ref[0, 2:3, lo:lo + OUT_BLOCK]
        o_ref[0, :, lo:lo + OUT_BLOCK] = x_ref[0, :, lo:lo + OUT_BLOCK] + gate * yt.T


def _out_proj(x, ret_t, att_t, mg_t, mod3, onw, w_out_t, tm):
    bsz, s, d = x.shape
    tok = lambda b, i: (b, 0, i)
    const = lambda b, i: (0, 0)
    return pl.pallas_call(
        _out_kernel,
        grid=(bsz, s // tm),
        in_specs=[pl.BlockSpec((1, tm, d), lambda b, i: (b, i, 0)),
                  pl.BlockSpec((1, ret_t.shape[1], tm), tok),
                  pl.BlockSpec((1, att_t.shape[1], tm), tok),
                  pl.BlockSpec((1, mg_t.shape[1], tm), tok),
                  pl.BlockSpec((1, 3, d), lambda b, i: (b, 0, 0)),
                  pl.BlockSpec(onw.shape, const),
                  pl.BlockSpec(w_out_t.shape, const)],
        out_specs=pl.BlockSpec((1, tm, d), lambda b, i: (b, i, 0)),
        out_shape=jax.ShapeDtypeStruct((bsz, s, d), x.dtype),
        compiler_params=_params(("parallel", "parallel")),
        name="out_proj",
    )(x, ret_t, att_t, mg_t, mod3, onw, w_out_t)


def _score_bound_ok(qn_w, qr_w, kn_w, kr_w):
    def sq(w, d):
        return d * jnp.max(jnp.square(w.astype(_F32)))
    bound2 = (sq(qn_w, MLA_NOPE) + sq(qr_w, MLA_ROPE)) * (sq(kn_w, MLA_NOPE) + sq(kr_w, MLA_ROPE))
    return (bound2 <= SCORE_BOUND_LOG2 ** 2).astype(jnp.int32).reshape(1)


def _col(v):
    return v.astype(_F32).reshape(-1, 1)


def _block_sizes(s):
    tm = min(512, s)
    tm_out = min(1024, s)
    tq = min(512, s)
    tk = min(1024, s // ATTN_BUFS)
    chunk = min(256, s)
    return tm, tm_out, tq, tk, chunk


def kernel(x, c, positions, norm_w, w_ada, b_ada, w_in, ret_decay_logit_fwd, ret_decay_logit_bwd, ret_gn_w, q_norm_w, w_uq, kv_norm_w, w_ukv, qn_nope_w, qn_rope_w, kn_nope_w, kn_rope_w, mla_out_norm_w, w_out):
    bsz, s, d = x.shape
    tm, tm_out, tq, tk, chunk = _block_sizes(s)
    assert s % tm == 0 and s % tm_out == 0 and s % chunk == 0 and d == w_in.shape[0]

    w_in_t = w_in.T.astype(_BF16)
    wq_t = w_uq.T.astype(_BF16)
    wkv_t = w_ukv.T.astype(_BF16)
    w_out_t = w_out.T.astype(_BF16)
    inv64 = (ROPE_BASE ** (-jnp.arange(0, RET_QK, 2, dtype=_F32) / RET_QK)).reshape(-1, 1)
    inv32 = (ROPE_BASE ** (-jnp.arange(0, MLA_ROPE, 2, dtype=_F32) / MLA_ROPE)).reshape(-1, 1)

    mod3 = _ada(c, w_ada, b_ada).reshape(bsz, 3, d)
    pos3 = positions.reshape(bsz, 1, s)

    q_mul = (MLA_NOPE + MLA_ROPE) ** -0.5 * LOG2E
    qn_w = qn_nope_w.astype(_F32) * q_mul
    qr_w = qn_rope_w.astype(_F32) * q_mul

    (rq_t, rk_t, rv_t, rg_t, mg_t, qt, k, vt) = _inproj(
        x, pos3, mod3, norm_w, w_in_t, inv64, inv32,
        _col(q_norm_w), _col(kv_norm_w), _col(kn_rope_w), wq_t, wkv_t,
        _col(qn_w), _col(qr_w), _col(kn_nope_w), tm)

    ret_t = _retention(ret_decay_logit_fwd.astype(_F32), ret_decay_logit_bwd.astype(_F32),
                       rq_t, rk_t, rv_t, rg_t,
                       ret_gn_w.astype(_F32).reshape(RET_HEADS, RET_V, 1), chunk)

    att_t = _attention(_score_bound_ok(qn_w, qr_w, kn_nope_w, kn_rope_w), qt, k, vt, tq, tk)

    return _out_proj(x, ret_t, att_t, mg_t, mod3, _col(mla_out_norm_w), w_out_t, tm_out)
```

```python
import functools

import jax
import jax.numpy as jnp
from jax import lax
from jax.experimental import pallas as pl
from jax.experimental.pallas import tpu as pltpu

RET_HEADS = 8
RET_QK = 64
RET_V = 128
MLA_HEADS = 8
MLA_NOPE = 64
MLA_ROPE = 32
MLA_V = 128
Q_LORA = 384
KV_LORA = 256
ROPE_BASE = 10000.0
EPS = 1e-6
HEAD_PAD = 128
LOG2E = 1.4426950408889634

_SIZES = (RET_HEADS * RET_QK, RET_HEADS * RET_QK, RET_HEADS * RET_V, RET_HEADS * RET_V,
          Q_LORA, KV_LORA, MLA_ROPE, MLA_HEADS * MLA_V)
_OFFS = tuple(sum(_SIZES[:i]) for i in range(len(_SIZES) + 1))

VMEM_LIMIT = 56 * 1024 * 1024

_NT = (((1,), (1,)), ((), ()))
_TN = (((0,), (0,)), ((), ()))
_F32 = jnp.float32
_BF16 = jnp.bfloat16


def _silu(v):
    u = 0.5 * v
    return u + u * jnp.tanh(u)


def _params(sem):
    return pltpu.CompilerParams(dimension_semantics=sem, vmem_limit_bytes=VMEM_LIMIT)


def _ada_kernel(c_ref, w_ref, b_ref, o_ref):
    a = _silu(c_ref[...]).astype(_BF16)
    o_ref[...] = jnp.dot(a, w_ref[...].astype(_BF16), preferred_element_type=_F32) + b_ref[...]


def _ada(c, w_ada, b_ada):
    bsz, d = c.shape
    n = w_ada.shape[1]
    tn = 1024
    return pl.pallas_call(
        _ada_kernel,
        grid=(n // tn,),
        in_specs=[pl.BlockSpec((bsz, d), lambda j: (0, 0)),
                  pl.BlockSpec((d, tn), lambda j: (0, j)),
                  pl.BlockSpec((1, tn), lambda j: (0, j))],
        out_specs=pl.BlockSpec((bsz, tn), lambda j: (0, j)),
        out_shape=jax.ShapeDtypeStruct((bsz, n), _F32),
        compiler_params=_params(("arbitrary",)),
        name="ada_mod",
    )(c, w_ada, b_ada.reshape(1, n))


def _inproj_kernel(x_ref, pos_ref, mod_ref, nw_ref, w_ref, inv64_ref, inv32_ref,
                   qnw_ref, kvnw_ref, krw_ref, wq_ref, wkv_ref, qnn_ref, qrn_ref, knn_ref, shift_ref,
                   rq_ref, rk_ref, rv_ref, rg_ref, mg_ref,
                   qt_ref, k_ref, vt_ref):
    tm = x_ref.shape[1]
    x = x_ref[0]
    shift = mod_ref[0, 0:1, :]
    scale = mod_ref[0, 1:2, :]
    gain = nw_ref[...] * (1.0 + scale)
    h = (x * lax.rsqrt(jnp.mean(x * x, axis=-1, keepdims=True) + EPS) * gain + shift).astype(_BF16)

    def proj(i, j=None):
        return lax.dot_general(w_ref[_OFFS[i]:_OFFS[i + 1 if j is None else j], :], h, _NT,
                               preferred_element_type=_F32)

    def rms(v, w_col):
        return v * lax.rsqrt(jnp.mean(v * v, axis=0, keepdims=True) + EPS) * w_col

    def rot(v, cos, sin):
        n = v.shape[0] // 2
        return v[:n] * cos - v[n:] * sin, v[n:] * cos + v[:n] * sin

    pos = pos_ref[0].astype(_F32)
    ang64 = inv64_ref[...] * pos
    cos64, sin64 = jnp.cos(ang64), jnp.sin(ang64)
    ang32 = inv32_ref[...] * pos
    cos32, sin32 = jnp.cos(ang32), jnp.sin(ang32)

    lat = proj(4, 7)
    cq = rms(lat[:Q_LORA], qnw_ref[...]).astype(_BF16)
    ckv = rms(lat[Q_LORA:Q_LORA + KV_LORA], kvnw_ref[...]).astype(_BF16)
    kr = jnp.concatenate(rot(rms(lat[Q_LORA + KV_LORA:], krw_ref[...]), cos32, sin32), axis=0)
    q_all = jnp.dot(wq_ref[...], cq, preferred_element_type=_F32)
    kv_all = jnp.dot(wkv_ref[...], ckv, preferred_element_type=_F32)
    pad_row = lax.broadcasted_iota(jnp.int32, (HEAD_PAD - MLA_NOPE - MLA_ROPE, tm), 0)
    qpad = jnp.where(pad_row == 0, shift_ref[...], 0.0)
    kpad = jnp.where(pad_row == 0, 1.0, 0.0)
    hq = MLA_NOPE + MLA_ROPE
    hkv = MLA_NOPE + MLA_V
    for hd in range(MLA_HEADS):
        qn = rms(q_all[hd * hq: hd * hq + MLA_NOPE], qnn_ref[...])
        qr = rot(rms(q_all[hd * hq + MLA_NOPE: (hd + 1) * hq], qrn_ref[...]), cos32, sin32)
        qt_ref[0, hd] = jnp.concatenate([qn, qr[0], qr[1], qpad], axis=0).astype(_BF16)
        kn = rms(kv_all[hd * hkv: hd * hkv + MLA_NOPE], knn_ref[...])
        kh = jnp.concatenate([kn, kr, kpad], axis=0)
        k_ref[0, hd] = kh.T.astype(_BF16)
        vt_ref[0, hd] = kv_all[hd * hkv + MLA_NOPE: (hd + 1) * hkv].astype(_BF16)

    rg_ref[0] = _silu(proj(3)).astype(_BF16)
    mg_ref[0] = _silu(proj(7)).astype(_BF16)
    qk = proj(0, 2)
    k_mul = RET_QK ** -0.5
    for src, dst, cs, sn in ((0, rq_ref, cos64, sin64), (1, rk_ref, cos64 * k_mul, sin64 * k_mul)):
        for hd in range(RET_HEADS):
            lo = _OFFS[src] + hd * RET_QK
            r1, r2 = rot(qk[lo: lo + RET_QK], cs, sn)
            dst[0, hd * RET_QK: hd * RET_QK + RET_QK // 2, :] = r1.astype(_BF16)
            dst[0, hd * RET_QK + RET_QK // 2: (hd + 1) * RET_QK, :] = r2.astype(_BF16)
    rv_ref[0] = proj(2).astype(_BF16)


def _inproj(x, pos3, mod3, norm_w, w_in_t, inv64, inv32, qnw, kvnw, krw, wq_t, wkv_t, qnn, qrn, knn, shift, tm):
    bsz, s, d = x.shape
    const = lambda b, i: (0, 0)
    tok = lambda b, i: (b, 0, i)
    head_t = lambda b, i: (b, 0, 0, i)
    sizes = (_SIZES[0], _SIZES[1], _SIZES[2], _SIZES[3], _SIZES[7])
    out_shape = [jax.ShapeDtypeStruct((bsz, n, s), _BF16) for n in sizes]
    out_specs = [pl.BlockSpec((1, n, tm), tok) for n in sizes]
    out_shape += [jax.ShapeDtypeStruct((bsz, MLA_HEADS, HEAD_PAD, s), _BF16),
                  jax.ShapeDtypeStruct((bsz, MLA_HEADS, s, HEAD_PAD), _BF16),
                  jax.ShapeDtypeStruct((bsz, MLA_HEADS, MLA_V, s), _BF16)]
    out_specs += [pl.BlockSpec((1, MLA_HEADS, HEAD_PAD, tm), head_t),
                  pl.BlockSpec((1, MLA_HEADS, tm, HEAD_PAD), lambda b, i: (b, 0, i, 0)),
                  pl.BlockSpec((1, MLA_HEADS, MLA_V, tm), head_t)]
    consts = (w_in_t, inv64, inv32, qnw, kvnw, krw, wq_t, wkv_t, qnn, qrn, knn, shift)
    return pl.pallas_call(
        _inproj_kernel,
        grid=(bsz, s // tm),
        in_specs=[pl.BlockSpec((1, tm, d), lambda b, i: (b, i, 0)),
                  pl.BlockSpec((1, 1, tm), tok),
                  pl.BlockSpec((1, 3, d), lambda b, i: (b, 0, 0)),
                  pl.BlockSpec((1, d), const)] + [pl.BlockSpec(a.shape, const) for a in consts],
        out_specs=out_specs,
        out_shape=out_shape,
        compiler_params=_params(("parallel", "parallel")),
        name="in_proj",
    )(x, pos3, mod3, norm_w.reshape(1, d), *consts)


RET_UNROLL = 32


def _log_sigmoid(v):
    return jnp.minimum(v, 0.0) - jnp.log(1.0 + jnp.exp(-jnp.abs(v)))


def _retention_kernel(lf_ref, lb_ref, q_ref, k_ref, v_ref, g_ref, gnw_ref, o_ref,
                      kv_ref, st_ref, *, chunk, n_chunks):
    hd = pl.program_id(1)
    c = chunk
    lgf = _log_sigmoid(jnp.full((1, 1), lf_ref[hd], _F32))
    lgb = _log_sigmoid(jnp.full((1, 1), lb_ref[hd], _F32))

    pos_l = lax.broadcasted_iota(jnp.int32, (1, c), 1).astype(_F32)
    kdec_f = jnp.exp(lgf * (c - 1.0 - pos_l))
    kdec_b = jnp.exp(lgb * pos_l)
    qdec_f = jnp.exp(lgf * (pos_l + 1.0))
    qdec_b = jnp.exp(lgb * (c - pos_l))
    jj = lax.broadcasted_iota(jnp.int32, (c, c), 0).astype(_F32)
    ii = lax.broadcasted_iota(jnp.int32, (c, c), 1).astype(_F32)
    dmat = jnp.where(ii >= jj, jnp.exp(lgf * jnp.maximum(ii - jj, 0.0)),
                     jnp.exp(lgb * jnp.maximum(jj - ii, 0.0)))
    lane = lax.broadcasted_iota(jnp.int32, (1, 2 * RET_QK), 1)
    is_f = lane < RET_QK
    sdec = jnp.where(is_f, jnp.exp(lgf * c), jnp.exp(lgb * c))

    def chunk_ds(i):
        return pl.ds(pl.multiple_of(i * c, c), c)

    def kv_body(i, carry):
        ds = chunk_ds(i)
        kc = k_ref[0, :, ds].astype(_F32)
        kfb = jnp.concatenate([kc * kdec_f, kc * kdec_b], axis=0).astype(_BF16)
        kv_ref[i] = lax.dot_general(v_ref[0, :, ds], kfb, _NT, preferred_element_type=_F32)
        return carry

    unroll = min(n_chunks, RET_UNROLL)
    lax.fori_loop(0, n_chunks, kv_body, 0, unroll=unroll)

    def bwd_body(t, sb):
        i = n_chunks - 1 - t
        st_ref[i] = sb
        return sb * sdec + kv_ref[i]

    lax.fori_loop(0, n_chunks, bwd_body, jnp.zeros((RET_V, 2 * RET_QK), _F32))

    gnw = gnw_ref[0]

    def out_body(i, sf):
        ds = chunk_ds(i)
        qc = q_ref[0, :, ds]
        kc = k_ref[0, :, ds]
        st = lax.dot_general(kc, qc, _TN, preferred_element_type=_F32)
        pt = (st * dmat).astype(_BF16)
        qf = qc.astype(_F32)
        qfb = jnp.concatenate([qf * qdec_f, qf * qdec_b], axis=0).astype(_BF16)
        state = jnp.where(is_f, sf, st_ref[i]).astype(_BF16)
        o = (jnp.dot(v_ref[0, :, ds], pt, preferred_element_type=_F32)
             + jnp.dot(state, qfb, preferred_element_type=_F32))
        mu = jnp.mean(o, axis=0, keepdims=True)
        dlt = o - mu
        var = jnp.mean(dlt * dlt, axis=0, keepdims=True)
        on = dlt * lax.rsqrt(var + EPS) * gnw
        o_ref[0, :, ds] = on.astype(_BF16) * g_ref[0, :, ds]
        return sf * sdec + kv_ref[i]

    lax.fori_loop(0, n_chunks, out_body, jnp.zeros((RET_V, 2 * RET_QK), _F32), unroll=unroll)


def _retention(lf, lb, rq_t, rk_t, rv_t, rg_t, gnw3, chunk):
    bsz, _, s = rq_t.shape
    n_chunks = s // chunk
    smem = pl.BlockSpec(memory_space=pltpu.SMEM)
    return pl.pallas_call(
        functools.partial(_retention_kernel, chunk=chunk, n_chunks=n_chunks),
        grid=(bsz, RET_HEADS),
        in_specs=[smem, smem,
                  pl.BlockSpec((1, RET_QK, s), lambda b, h: (b, h, 0)),
                  pl.BlockSpec((1, RET_QK, s), lambda b, h: (b, h, 0)),
                  pl.BlockSpec((1, RET_V, s), lambda b, h: (b, h, 0)),
                  pl.BlockSpec((1, RET_V, s), lambda b, h: (b, h, 0)),
                  pl.BlockSpec((1, RET_V, 1), lambda b, h: (h, 0, 0))],
        out_specs=pl.BlockSpec((1, RET_V, s), lambda b, h: (b, h, 0)),
        out_shape=jax.ShapeDtypeStruct((bsz, RET_HEADS * RET_V, s), _BF16),
        scratch_shapes=[pltpu.VMEM((n_chunks, RET_V, 2 * RET_QK), _F32),
                        pltpu.VMEM((n_chunks, RET_V, 2 * RET_QK), _F32)],
        compiler_params=_params(("parallel", "parallel")),
        name="retention",
    )(lf, lb, rq_t, rk_t, rv_t, rg_t, gnw3)


def _attn_kernel(fast_ref, qt_ref, k_ref, vt_ref, o_ref, *bufs, tq, tk, tqf, tkf, n_q, n_kv):
    nb = len(bufs)
    fast = fast_ref[0]

    def q_ds(qi):
        return pl.ds(pl.multiple_of(qi * tq, tq), tq)

    def kv_ds(j):
        return pl.ds(pl.multiple_of(j * tk, tk), tk)

    def write(qi, l, acc):
        o_ref[0, :, q_ds(qi)] = (acc * (1.0 / l)).astype(_BF16)

    @pl.when(fast != 0)
    def _fast_path():
        def body(qi, carry):
            qds = pl.ds(pl.multiple_of(qi * tqf, tqf), tqf)
            qt = qt_ref[0, 0, :, qds]
            l = jnp.zeros((1, tqf), _F32)
            acc = jnp.zeros((MLA_V, tqf), _F32)
            for j in range(n_kv * tk // tkf):
                ds = pl.ds(j * tkf, tkf)
                st = jnp.dot(k_ref[0, 0, ds, :], qt, preferred_element_type=_F32)
                p = jnp.exp2(st)
                l = l + jnp.sum(p, axis=0, keepdims=True)
                acc = acc + jnp.dot(vt_ref[0, 0, :, ds], p.astype(_BF16),
                                    preferred_element_type=_F32)
            o_ref[0, :, qds] = (acc * (1.0 / l)).astype(_BF16)
            return carry

        lax.fori_loop(0, n_q * tq // tqf, body, 0)

    @pl.when(fast == 0)
    def _safe_path():
        def scores(qi, j, m, s_ref):
            st = jnp.dot(k_ref[0, 0, kv_ds(j), :], qt_ref[0, 0, :, q_ds(qi)],
                         preferred_element_type=_F32)
            s_ref[...] = st
            return jnp.maximum(m, jnp.max(st, axis=0, keepdims=True))

        def accumulate(j, m_prev, m, l, acc, s_ref):
            alpha = jnp.exp2(m_prev - m)
            p = jnp.exp2(s_ref[...] - m)
            l = alpha * l + jnp.sum(p, axis=0, keepdims=True)
            acc = alpha * acc + jnp.dot(vt_ref[0, 0, :, kv_ds(j)], p.astype(_BF16),
                                        preferred_element_type=_F32)
            return l, acc

        neg_inf = jnp.full((1, tq), -jnp.inf, _F32)

        def body(qi, carry):
            ms = [neg_inf] + list(carry)
            nxt = []
            qn = jnp.minimum(qi + 1, n_q - 1)
            l = jnp.zeros((1, tq), _F32)
            acc = jnp.zeros((MLA_V, tq), _F32)
            for j in range(n_kv):
                t = j + ATTN_AHEAD
                if t < n_kv:
                    ms.append(scores(qi, t, ms[-1], bufs[t % nb]))
                else:
                    nxt.append(scores(qn, t - n_kv, nxt[-1] if nxt else neg_inf, bufs[t % nb]))
                l, acc = accumulate(j, ms[j], ms[j + 1], l, acc, bufs[j % nb])
            write(qi, l, acc)
            return tuple(nxt)

        first = []
        for j in range(ATTN_AHEAD):
            first.append(scores(0, j, first[-1] if first else neg_inf, bufs[j]))
        lax.fori_loop(0, n_q, body, tuple(first))


ATTN_AHEAD = 2
ATTN_BUFS = 4
SCORE_BOUND_LOG2 = 60.0
ATTN_FAST_TK = 4096
ATTN_FAST_TQ = 1024


def _attention(fast, qt, k, vt, tq, tk):
    bsz, nh, _, s = qt.shape
    n_kv = s // tk
    assert s % tq == 0 and s % tk == 0 and n_kv % ATTN_BUFS == 0 and ATTN_AHEAD < ATTN_BUFS
    return pl.pallas_call(
        functools.partial(_attn_kernel, tq=tq, tk=tk, tqf=min(ATTN_FAST_TQ, s), tkf=min(ATTN_FAST_TK, s),
                          n_q=s // tq, n_kv=s // tk),
        grid=(bsz, nh),
        in_specs=[pl.BlockSpec(memory_space=pltpu.SMEM),
                  pl.BlockSpec((1, 1, HEAD_PAD, s), lambda b, h: (b, h, 0, 0)),
                  pl.BlockSpec((1, 1, s, HEAD_PAD), lambda b, h: (b, h, 0, 0)),
                  pl.BlockSpec((1, 1, MLA_V, s), lambda b, h: (b, h, 0, 0))],
        out_specs=pl.BlockSpec((1, MLA_V, s), lambda b, h: (b, h, 0)),
        out_shape=jax.ShapeDtypeStruct((bsz, nh * MLA_V, s), _BF16),
        scratch_shapes=[pltpu.VMEM((tk, tq), _F32)] * ATTN_BUFS,
        compiler_params=_params(("parallel", "parallel")),
        name="mla_attention",
    )(fast, qt, k, vt)


OUT_BLOCK = 256


def _out_kernel(x_ref, ret_ref, att_ref, mg_ref, mod_ref, onw_ref, w_ref, o_ref):
    att = att_ref[0].astype(_F32)
    mla = att * lax.rsqrt(jnp.mean(att * att, axis=0, keepdims=True) + EPS) * onw_ref[...]
    mla = (mla * mg_ref[0].astype(_F32)).astype(_BF16)
    cat = jnp.concatenate([ret_ref[0], mla], axis=0)
    d = w_ref.shape[0]
    for lo in range(0, d, OUT_BLOCK):
        yt = jnp.dot(w_ref[lo:lo + OUT_BLOCK, :], cat, preferred_element_type=_F32)
        gate = mod_ref[0, 2:3, lo:lo + OUT_BLOCK]
        o_ref[0, :, lo:lo + OUT_BLOCK] = x_ref[0, :, lo:lo + OUT_BLOCK] + gate * yt.T


def _out_proj(x, ret_t, att_t, mg_t, mod3, onw, w_out_t, tm):
    bsz, s, d = x.shape
    tok = lambda b, i: (b, 0, i)
    const = lambda b, i: (0, 0)
    return pl.pallas_call(
        _out_kernel,
        grid=(bsz, s // tm),
        in_specs=[pl.BlockSpec((1, tm, d), lambda b, i: (b, i, 0)),
                  pl.BlockSpec((1, ret_t.shape[1], tm), tok),
                  pl.BlockSpec((1, att_t.shape[1], tm), tok),
                  pl.BlockSpec((1, mg_t.shape[1], tm), tok),
                  pl.BlockSpec((1, 3, d), lambda b, i: (b, 0, 0)),
                  pl.BlockSpec(onw.shape, const),
                  pl.BlockSpec(w_out_t.shape, const)],
        out_specs=pl.BlockSpec((1, tm, d), lambda b, i: (b, i, 0)),
        out_shape=jax.ShapeDtypeStruct((bsz, s, d), x.dtype),
        compiler_params=_params(("parallel", "parallel")),
        name="out_proj",
    )(x, ret_t, att_t, mg_t, mod3, onw, w_out_t)


def _score_bound(qn_w, qr_w, kn_w, kr_w):
    def sq(w, d):
        return d * jnp.max(jnp.square(w.astype(_F32)))
    bound2 = (sq(qn_w, MLA_NOPE) + sq(qr_w, MLA_ROPE)) * (sq(kn_w, MLA_NOPE) + sq(kr_w, MLA_ROPE))
    ok = bound2 <= SCORE_BOUND_LOG2 ** 2
    shift = jnp.where(ok, -jnp.sqrt(bound2), 0.0).astype(_F32)
    return ok.astype(jnp.int32).reshape(1), shift.reshape(1, 1)


def _col(v):
    return v.astype(_F32).reshape(-1, 1)


def _block_sizes(s):
    tm = min(512, s)
    tm_out = min(1024, s)
    tq = min(512, s)
    tk = min(1024, s // ATTN_BUFS)
    chunk = min(256, s)
    return tm, tm_out, tq, tk, chunk


def kernel(x, c, positions, norm_w, w_ada, b_ada, w_in, ret_decay_logit_fwd, ret_decay_logit_bwd, ret_gn_w, q_norm_w, w_uq, kv_norm_w, w_ukv, qn_nope_w, qn_rope_w, kn_nope_w, kn_rope_w, mla_out_norm_w, w_out):
    bsz, s, d = x.shape
    tm, tm_out, tq, tk, chunk = _block_sizes(s)
    assert s % tm == 0 and s % tm_out == 0 and s % chunk == 0 and d == w_in.shape[0]

    w_in_t = w_in.T.astype(_BF16)
    wq_t = w_uq.T.astype(_BF16)
    wkv_t = w_ukv.T.astype(_BF16)
    w_out_t = w_out.T.astype(_BF16)
    inv64 = (ROPE_BASE ** (-jnp.arange(0, RET_QK, 2, dtype=_F32) / RET_QK)).reshape(-1, 1)
    inv32 = (ROPE_BASE ** (-jnp.arange(0, MLA_ROPE, 2, dtype=_F32) / MLA_ROPE)).reshape(-1, 1)

    mod3 = _ada(c, w_ada, b_ada).reshape(bsz, 3, d)
    pos3 = positions.reshape(bsz, 1, s)

    q_mul = (MLA_NOPE + MLA_ROPE) ** -0.5 * LOG2E
    qn_w = qn_nope_w.astype(_F32) * q_mul
    qr_w = qn_rope_w.astype(_F32) * q_mul
    fast, shift = _score_bound(qn_w, qr_w, kn_nope_w, kn_rope_w)

    (rq_t, rk_t, rv_t, rg_t, mg_t, qt, k, vt) = _inproj(
        x, pos3, mod3, norm_w, w_in_t, inv64, inv32,
        _col(q_norm_w), _col(kv_norm_w), _col(kn_rope_w), wq_t, wkv_t,
        _col(qn_w), _col(qr_w), _col(kn_nope_w), shift, tm)

    ret_t = _retention(ret_decay_logit_fwd.astype(_F32), ret_decay_logit_bwd.astype(_F32),
                       rq_t, rk_t, rv_t, rg_t,
                       ret_gn_w.astype(_F32).reshape(RET_HEADS, RET_V, 1), chunk)

    att_t = _attention(fast, qt, k, vt, tq, tk)

    return _out_proj(x, ret_t, att_t, mg_t, mod3, _col(mla_out_norm_w), w_out_t, tm_out)
```

```python
import functools

import jax
import jax.numpy as jnp
from jax import lax
from jax.experimental import pallas as pl
from jax.experimental.pallas import tpu as pltpu

RET_HEADS = 8
RET_QK = 64
RET_V = 128
MLA_HEADS = 8
MLA_NOPE = 64
MLA_ROPE = 32
MLA_V = 128
Q_LORA = 384
KV_LORA = 256
ROPE_BASE = 10000.0
EPS = 1e-6
HEAD_PAD = 128
LOG2E = 1.4426950408889634

_SIZES = (RET_HEADS * RET_QK, RET_HEADS * RET_QK, RET_HEADS * RET_V, RET_HEADS * RET_V,
          Q_LORA, KV_LORA, MLA_ROPE, MLA_HEADS * MLA_V)
_OFFS = tuple(sum(_SIZES[:i]) for i in range(len(_SIZES) + 1))

VMEM_LIMIT = 56 * 1024 * 1024

_NT = (((1,), (1,)), ((), ()))
_TN = (((0,), (0,)), ((), ()))
_F32 = jnp.float32
_BF16 = jnp.bfloat16


def _silu(v):
    u = 0.5 * v
    return u + u * jnp.tanh(u)


def _params(sem):
    return pltpu.CompilerParams(dimension_semantics=sem, vmem_limit_bytes=VMEM_LIMIT)


def _ada_kernel(c_ref, w_ref, b_ref, o_ref):
    a = _silu(c_ref[...]).astype(_BF16)
    o_ref[...] = jnp.dot(a, w_ref[...].astype(_BF16), preferred_element_type=_F32) + b_ref[...]


def _ada(c, w_ada, b_ada):
    bsz, d = c.shape
    n = w_ada.shape[1]
    tn = 1024
    return pl.pallas_call(
        _ada_kernel,
        grid=(n // tn,),
        in_specs=[pl.BlockSpec((bsz, d), lambda j: (0, 0)),
                  pl.BlockSpec((d, tn), lambda j: (0, j)),
                  pl.BlockSpec((1, tn), lambda j: (0, j))],
        out_specs=pl.BlockSpec((bsz, tn), lambda j: (0, j)),
        out_shape=jax.ShapeDtypeStruct((bsz, n), _F32),
        compiler_params=_params(("arbitrary",)),
        name="ada_mod",
    )(c, w_ada, b_ada.reshape(1, n))


def _inproj_kernel(x_ref, pos_ref, mod_ref, nw_ref, w_ref, inv64_ref, inv32_ref,
                   qnw_ref, kvnw_ref, krw_ref, wq_ref, wkv_ref, qnn_ref, qrn_ref, knn_ref, shift_ref,
                   rq_ref, rk_ref, rv_ref, rg_ref, mg_ref,
                   qt_ref, k_ref, vt_ref):
    tm = x_ref.shape[1]
    x = x_ref[0]
    shift = mod_ref[0, 0:1, :]
    scale = mod_ref[0, 1:2, :]
    gain = nw_ref[...] * (1.0 + scale)
    h = (x * lax.rsqrt(jnp.mean(x * x, axis=-1, keepdims=True) + EPS) * gain + shift).astype(_BF16)

    def proj(i, j=None):
        return lax.dot_general(w_ref[_OFFS[i]:_OFFS[i + 1 if j is None else j], :], h, _NT,
                               preferred_element_type=_F32)

    def rms(v, w_col):
        return v * lax.rsqrt(jnp.mean(v * v, axis=0, keepdims=True) + EPS) * w_col

    def rot(v, cos, sin):
        n = v.shape[0] // 2
        return v[:n] * cos - v[n:] * sin, v[n:] * cos + v[:n] * sin

    pos = pos_ref[0].astype(_F32)
    ang64 = inv64_ref[...] * pos
    cos64, sin64 = jnp.cos(ang64), jnp.sin(ang64)
    ang32 = inv32_ref[...] * pos
    cos32, sin32 = jnp.cos(ang32), jnp.sin(ang32)

    lat = proj(4, 7)
    cq = rms(lat[:Q_LORA], qnw_ref[...]).astype(_BF16)
    ckv = rms(lat[Q_LORA:Q_LORA + KV_LORA], kvnw_ref[...]).astype(_BF16)
    kr = jnp.concatenate(rot(rms(lat[Q_LORA + KV_LORA:], krw_ref[...]), cos32, sin32), axis=0)
    q_all = jnp.dot(wq_ref[...], cq, preferred_element_type=_F32)
    kv_all = jnp.dot(wkv_ref[...], ckv, preferred_element_type=_F32)
    pad_row = lax.broadcasted_iota(jnp.int32, (HEAD_PAD - MLA_NOPE - MLA_ROPE, tm), 0)
    qpad = jnp.where(pad_row == 0, shift_ref[...], 0.0)
    kpad = jnp.where(pad_row == 0, 1.0, 0.0)
    hq = MLA_NOPE + MLA_ROPE
    hkv = MLA_NOPE + MLA_V
    for hd in range(MLA_HEADS):
        qn = rms(q_all[hd * hq: hd * hq + MLA_NOPE], qnn_ref[...])
        qr = rot(rms(q_all[hd * hq + MLA_NOPE: (hd + 1) * hq], qrn_ref[...]), cos32, sin32)
        qt_ref[0, hd] = jnp.concatenate([qn, qr[0], qr[1], qpad], axis=0).astype(_BF16)
        kn = rms(kv_all[hd * hkv: hd * hkv + MLA_NOPE], knn_ref[...])
        kh = jnp.concatenate([kn, kr, kpad], axis=0)
        k_ref[0, hd] = kh.T.astype(_BF16)
        vt_ref[0, hd] = kv_all[hd * hkv + MLA_NOPE: (hd + 1) * hkv].astype(_BF16)

    rg_ref[0] = _silu(proj(3)).astype(_BF16)
    mg_ref[0] = _silu(proj(7)).astype(_BF16)
    qk = proj(0, 2)
    k_mul = RET_QK ** -0.5
    for src, dst, cs, sn in ((0, rq_ref, cos64, sin64), (1, rk_ref, cos64 * k_mul, sin64 * k_mul)):
        for hd in range(RET_HEADS):
            lo = _OFFS[src] + hd * RET_QK
            r1, r2 = rot(qk[lo: lo + RET_QK], cs, sn)
            dst[0, hd * RET_QK: hd * RET_QK + RET_QK // 2, :] = r1.astype(_BF16)
            dst[0, hd * RET_QK + RET_QK // 2: (hd + 1) * RET_QK, :] = r2.astype(_BF16)
    rv_ref[0] = proj(2).astype(_BF16)


def _inproj(x, pos3, mod3, norm_w, w_in_t, inv64, inv32, qnw, kvnw, krw, wq_t, wkv_t, qnn, qrn, knn, shift, tm):
    bsz, s, d = x.shape
    const = lambda b, i: (0, 0)
    tok = lambda b, i: (b, 0, i)
    head_t = lambda b, i: (b, 0, 0, i)
    sizes = (_SIZES[0], _SIZES[1], _SIZES[2], _SIZES[3], _SIZES[7])
    out_shape = [jax.ShapeDtypeStruct((bsz, n, s), _BF16) for n in sizes]
    out_specs = [pl.BlockSpec((1, n, tm), tok) for n in sizes]
    out_shape += [jax.ShapeDtypeStruct((bsz, MLA_HEADS, HEAD_PAD, s), _BF16),
                  jax.ShapeDtypeStruct((bsz, MLA_HEADS, s, HEAD_PAD), _BF16),
                  jax.ShapeDtypeStruct((bsz, MLA_HEADS, MLA_V, s), _BF16)]
    out_specs += [pl.BlockSpec((1, MLA_HEADS, HEAD_PAD, tm), head_t),
                  pl.BlockSpec((1, MLA_HEADS, tm, HEAD_PAD), lambda b, i: (b, 0, i, 0)),
                  pl.BlockSpec((1, MLA_HEADS, MLA_V, tm), head_t)]
    consts = (w_in_t, inv64, inv32, qnw, kvnw, krw, wq_t, wkv_t, qnn, qrn, knn, shift)
    return pl.pallas_call(
        _inproj_kernel,
        grid=(bsz, s // tm),
        in_specs=[pl.BlockSpec((1, tm, d), lambda b, i: (b, i, 0)),
                  pl.BlockSpec((1, 1, tm), tok),
                  pl.BlockSpec((1, 3, d), lambda b, i: (b, 0, 0)),
                  pl.BlockSpec((1, d), const)] + [pl.BlockSpec(a.shape, const) for a in consts],
        out_specs=out_specs,
        out_shape=out_shape,
        compiler_params=_params(("parallel", "parallel")),
        name="in_proj",
    )(x, pos3, mod3, norm_w.reshape(1, d), *consts)


RET_UNROLL = 32


def _log_sigmoid(v):
    return jnp.minimum(v, 0.0) - jnp.log(1.0 + jnp.exp(-jnp.abs(v)))


def _retention_kernel(lf_ref, lb_ref, q_ref, k_ref, v_ref, g_ref, gnw_ref, o_ref,
                      kv_ref, st_ref, *, chunk, n_chunks):
    hd = pl.program_id(1)
    c = chunk
    lgf = _log_sigmoid(jnp.full((1, 1), lf_ref[hd], _F32))
    lgb = _log_sigmoid(jnp.full((1, 1), lb_ref[hd], _F32))

    pos_l = lax.broadcasted_iota(jnp.int32, (1, c), 1).astype(_F32)
    kdec_f = jnp.exp(lgf * (c - 1.0 - pos_l))
    kdec_b = jnp.exp(lgb * pos_l)
    qdec_f = jnp.exp(lgf * (pos_l + 1.0))
    qdec_b = jnp.exp(lgb * (c - pos_l))
    jj = lax.broadcasted_iota(jnp.int32, (c, c), 0).astype(_F32)
    ii = lax.broadcasted_iota(jnp.int32, (c, c), 1).astype(_F32)
    dmat = jnp.where(ii >= jj, jnp.exp(lgf * jnp.maximum(ii - jj, 0.0)),
                     jnp.exp(lgb * jnp.maximum(jj - ii, 0.0)))
    lane = lax.broadcasted_iota(jnp.int32, (1, 2 * RET_QK), 1)
    is_f = lane < RET_QK
    sdec = jnp.where(is_f, jnp.exp(lgf * c), jnp.exp(lgb * c))

    def chunk_ds(i):
        return pl.ds(pl.multiple_of(i * c, c), c)

    def kv_body(i, carry):
        ds = chunk_ds(i)
        kc = k_ref[0, :, ds].astype(_F32)
        kfb = jnp.concatenate([kc * kdec_f, kc * kdec_b], axis=0).astype(_BF16)
        kv_ref[i] = lax.dot_general(v_ref[0, :, ds], kfb, _NT, preferred_element_type=_F32)
        return carry

    unroll = min(n_chunks, RET_UNROLL)
    lax.fori_loop(0, n_chunks, kv_body, 0, unroll=unroll)

    def bwd_body(t, sb):
        i = n_chunks - 1 - t
        st_ref[i] = sb
        return sb * sdec + kv_ref[i]

    lax.fori_loop(0, n_chunks, bwd_body, jnp.zeros((RET_V, 2 * RET_QK), _F32))

    gnw = gnw_ref[0]

    def out_body(i, sf):
        ds = chunk_ds(i)
        qc = q_ref[0, :, ds]
        kc = k_ref[0, :, ds]
        st = lax.dot_general(kc, qc, _TN, preferred_element_type=_F32)
        pt = (st * dmat).astype(_BF16)
        qf = qc.astype(_F32)
        qfb = jnp.concatenate([qf * qdec_f, qf * qdec_b], axis=0).astype(_BF16)
        state = jnp.where(is_f, sf, st_ref[i]).astype(_BF16)
        o = (jnp.dot(v_ref[0, :, ds], pt, preferred_element_type=_F32)
             + jnp.dot(state, qfb, preferred_element_type=_F32))
        mu = jnp.mean(o, axis=0, keepdims=True)
        dlt = o - mu
        var = jnp.mean(dlt * dlt, axis=0, keepdims=True)
        on = dlt * lax.rsqrt(var + EPS) * gnw
        o_ref[0, :, ds] = on.astype(_BF16) * g_ref[0, :, ds]
        return sf * sdec + kv_ref[i]

    lax.fori_loop(0, n_chunks, out_body, jnp.zeros((RET_V, 2 * RET_QK), _F32), unroll=unroll)


def _retention(lf, lb, rq_t, rk_t, rv_t, rg_t, gnw3, chunk):
    bsz, _, s = rq_t.shape
    n_chunks = s // chunk
    smem = pl.BlockSpec(memory_space=pltpu.SMEM)
    return pl.pallas_call(
        functools.partial(_retention_kernel, chunk=chunk, n_chunks=n_chunks),
        grid=(bsz, RET_HEADS),
        in_specs=[smem, smem,
                  pl.BlockSpec((1, RET_QK, s), lambda b, h: (b, h, 0)),
                  pl.BlockSpec((1, RET_QK, s), lambda b, h: (b, h, 0)),
                  pl.BlockSpec((1, RET_V, s), lambda b, h: (b, h, 0)),
                  pl.BlockSpec((1, RET_V, s), lambda b, h: (b, h, 0)),
                  pl.BlockSpec((1, RET_V, 1), lambda b, h: (h, 0, 0))],
        out_specs=pl.BlockSpec((1, RET_V, s), lambda b, h: (b, h, 0)),
        out_shape=jax.ShapeDtypeStruct((bsz, RET_HEADS * RET_V, s), _BF16),
        scratch_shapes=[pltpu.VMEM((n_chunks, RET_V, 2 * RET_QK), _F32),
                        pltpu.VMEM((n_chunks, RET_V, 2 * RET_QK), _F32)],
        compiler_params=_params(("parallel", "parallel")),
        name="retention",
    )(lf, lb, rq_t, rk_t, rv_t, rg_t, gnw3)


def _attn_kernel(fast_ref, qt_ref, k_ref, vt_ref, mg_ref, onw_ref, o_ref, ssq_ref, *bufs,
                 tq, tk, tqf, tkf, n_q, n_kv):
    nb = len(bufs)
    fast = fast_ref[0]

    def q_ds(qi):
        return pl.ds(pl.multiple_of(qi * tq, tq), tq)

    def kv_ds(j):
        return pl.ds(pl.multiple_of(j * tk, tk), tk)

    def emit(ds, l, acc):
        att = acc * (1.0 / l)
        ssq_ref[0, 0, :, ds] = jnp.sum(att * att, axis=0, keepdims=True)
        o_ref[0, :, ds] = (att * onw_ref[...] * mg_ref[0, :, ds].astype(_F32)).astype(_BF16)

    def write(qi, l, acc):
        emit(q_ds(qi), l, acc)

    @pl.when(fast != 0)
    def _fast_path():
        def body(qi, carry):
            qds = pl.ds(pl.multiple_of(qi * tqf, tqf), tqf)
            qt = qt_ref[0, 0, :, qds]
            l = jnp.zeros((1, tqf), _F32)
            acc = jnp.zeros((MLA_V, tqf), _F32)
            for j in range(n_kv * tk // tkf):
                ds = pl.ds(j * tkf, tkf)
                st = jnp.dot(k_ref[0, 0, ds, :], qt, preferred_element_type=_F32)
                p = jnp.exp2(st)
                l = l + jnp.sum(p, axis=0, keepdims=True)
                acc = acc + jnp.dot(vt_ref[0, 0, :, ds], p.astype(_BF16),
                                    preferred_element_type=_F32)
            emit(qds, l, acc)
            return carry

        lax.fori_loop(0, n_q * tq // tqf, body, 0)

    @pl.when(fast == 0)
    def _safe_path():
        def scores(qi, j, m, s_ref):
            st = jnp.dot(k_ref[0, 0, kv_ds(j), :], qt_ref[0, 0, :, q_ds(qi)],
                         preferred_element_type=_F32)
            s_ref[...] = st
            return jnp.maximum(m, jnp.max(st, axis=0, keepdims=True))

        def accumulate(j, m_prev, m, l, acc, s_ref):
            alpha = jnp.exp2(m_prev - m)
            p = jnp.exp2(s_ref[...] - m)
            l = alpha * l + jnp.sum(p, axis=0, keepdims=True)
            acc = alpha * acc + jnp.dot(vt_ref[0, 0, :, kv_ds(j)], p.astype(_BF16),
                                        preferred_element_type=_F32)
            return l, acc

        neg_inf = jnp.full((1, tq), -jnp.inf, _F32)

        def body(qi, carry):
            ms = [neg_inf] + list(carry)
            nxt = []
            qn = jnp.minimum(qi + 1, n_q - 1)
            l = jnp.zeros((1, tq), _F32)
            acc = jnp.zeros((MLA_V, tq), _F32)
            for j in range(n_kv):
                t = j + ATTN_AHEAD
                if t < n_kv:
                    ms.append(scores(qi, t, ms[-1], bufs[t % nb]))
                else:
                    nxt.append(scores(qn, t - n_kv, nxt[-1] if nxt else neg_inf, bufs[t % nb]))
                l, acc = accumulate(j, ms[j], ms[j + 1], l, acc, bufs[j % nb])
            write(qi, l, acc)
            return tuple(nxt)

        first = []
        for j in range(ATTN_AHEAD):
            first.append(scores(0, j, first[-1] if first else neg_inf, bufs[j]))
        lax.fori_loop(0, n_q, body, tuple(first))


ATTN_AHEAD = 2
ATTN_BUFS = 4
SCORE_BOUND_LOG2 = 60.0
ATTN_FAST_TK = 4096
ATTN_FAST_TQ = 1024


def _attention(fast, qt, k, vt, mg_t, onw, tq, tk):
    bsz, nh, _, s = qt.shape
    n_kv = s // tk
    assert s % tq == 0 and s % tk == 0 and n_kv % ATTN_BUFS == 0 and ATTN_AHEAD < ATTN_BUFS
    return pl.pallas_call(
        functools.partial(_attn_kernel, tq=tq, tk=tk, tqf=min(ATTN_FAST_TQ, s), tkf=min(ATTN_FAST_TK, s),
                          n_q=s // tq, n_kv=s // tk),
        grid=(bsz, nh),
        in_specs=[pl.BlockSpec(memory_space=pltpu.SMEM),
                  pl.BlockSpec((1, 1, HEAD_PAD, s), lambda b, h: (b, h, 0, 0)),
                  pl.BlockSpec((1, 1, s, HEAD_PAD), lambda b, h: (b, h, 0, 0)),
                  pl.BlockSpec((1, 1, MLA_V, s), lambda b, h: (b, h, 0, 0)),
                  pl.BlockSpec((1, MLA_V, s), lambda b, h: (b, h, 0)),
                  pl.BlockSpec((MLA_V, 1), lambda b, h: (h, 0))],
        out_specs=[pl.BlockSpec((1, MLA_V, s), lambda b, h: (b, h, 0)),
                   pl.BlockSpec((1, 1, 1, s), lambda b, h: (b, h, 0, 0))],
        out_shape=[jax.ShapeDtypeStruct((bsz, nh * MLA_V, s), _BF16),
                   jax.ShapeDtypeStruct((bsz, nh, 1, s), _F32)],
        scratch_shapes=[pltpu.VMEM((tk, tq), _F32)] * ATTN_BUFS,
        compiler_params=_params(("parallel", "parallel")),
        name="mla_attention",
    )(fast, qt, k, vt, mg_t, onw)


OUT_BLOCK = 256


def _out_kernel(x_ref, ret_ref, att_ref, ssq_ref, mod_ref, w_ref, o_ref):
    n_att = att_ref.shape[1]
    ms = jnp.sum(ssq_ref[0, :, 0, :], axis=0, keepdims=True) * (1.0 / n_att)
    mla = (att_ref[0].astype(_F32) * lax.rsqrt(ms + EPS)).astype(_BF16)
    cat = jnp.concatenate([ret_ref[0], mla], axis=0)
    d = w_ref.shape[0]
    for lo in range(0, d, OUT_BLOCK):
        yt = jnp.dot(w_ref[lo:lo + OUT_BLOCK, :], cat, preferred_element_type=_F32)
        gate = mod_ref[0, 2:3, lo:lo + OUT_BLOCK]
        o_ref[0, :, lo:lo + OUT_BLOCK] = x_ref[0, :, lo:lo + OUT_BLOCK] + gate * yt.T


def _out_proj(x, ret_t, att_t, ssq, mod3, w_out_t, tm):
    bsz, s, d = x.shape
    tok = lambda b, i: (b, 0, i)
    const = lambda b, i: (0, 0)
    return pl.pallas_call(
        _out_kernel,
        grid=(bsz, s // tm),
        in_specs=[pl.BlockSpec((1, tm, d), lambda b, i: (b, i, 0)),
                  pl.BlockSpec((1, ret_t.shape[1], tm), tok),
                  pl.BlockSpec((1, att_t.shape[1], tm), tok),
                  pl.BlockSpec((1, ssq.shape[1], 1, tm), lambda b, i: (b, 0, 0, i)),
                  pl.BlockSpec((1, 3, d), lambda b, i: (b, 0, 0)),
                  pl.BlockSpec(w_out_t.shape, const)],
        out_specs=pl.BlockSpec((1, tm, d), lambda b, i: (b, i, 0)),
        out_shape=jax.ShapeDtypeStruct((bsz, s, d), x.dtype),
        compiler_params=_params(("parallel", "parallel")),
        name="out_proj",
    )(x, ret_t, att_t, ssq, mod3, w_out_t)


def _score_bound(qn_w, qr_w, kn_w, kr_w):
    def sq(w, d):
        return d * jnp.max(jnp.square(w.astype(_F32)))
    bound2 = (sq(qn_w, MLA_NOPE) + sq(qr_w, MLA_ROPE)) * (sq(kn_w, MLA_NOPE) + sq(kr_w, MLA_ROPE))
    ok = bound2 <= SCORE_BOUND_LOG2 ** 2
    shift = jnp.where(ok, -jnp.sqrt(bound2), 0.0).astype(_F32)
    return ok.astype(jnp.int32).reshape(1), shift.reshape(1, 1)


def _col(v):
    return v.astype(_F32).reshape(-1, 1)


def _block_sizes(s):
    tm = min(512, s)
    tm_out = min(1024, s)
    tq = min(512, s)
    tk = min(1024, s // ATTN_BUFS)
    chunk = min(256, s)
    return tm, tm_out, tq, tk, chunk


def kernel(x, c, positions, norm_w, w_ada, b_ada, w_in, ret_decay_logit_fwd, ret_decay_logit_bwd, ret_gn_w, q_norm_w, w_uq, kv_norm_w, w_ukv, qn_nope_w, qn_rope_w, kn_nope_w, kn_rope_w, mla_out_norm_w, w_out):
    bsz, s, d = x.shape
    tm, tm_out, tq, tk, chunk = _block_sizes(s)
    assert s % tm == 0 and s % tm_out == 0 and s % chunk == 0 and d == w_in.shape[0]

    w_in_t = w_in.T.astype(_BF16)
    wq_t = w_uq.T.astype(_BF16)
    wkv_t = w_ukv.T.astype(_BF16)
    w_out_t = w_out.T.astype(_BF16)
    inv64 = (ROPE_BASE ** (-jnp.arange(0, RET_QK, 2, dtype=_F32) / RET_QK)).reshape(-1, 1)
    inv32 = (ROPE_BASE ** (-jnp.arange(0, MLA_ROPE, 2, dtype=_F32) / MLA_ROPE)).reshape(-1, 1)

    mod3 = _ada(c, w_ada, b_ada).reshape(bsz, 3, d)
    pos3 = positions.reshape(bsz, 1, s)

    q_mul = (MLA_NOPE + MLA_ROPE) ** -0.5 * LOG2E
    qn_w = qn_nope_w.astype(_F32) * q_mul
    qr_w = qn_rope_w.astype(_F32) * q_mul
    fast, shift = _score_bound(qn_w, qr_w, kn_nope_w, kn_rope_w)

    (rq_t, rk_t, rv_t, rg_t, mg_t, qt, k, vt) = _inproj(
        x, pos3, mod3, norm_w, w_in_t, inv64, inv32,
        _col(q_norm_w), _col(kv_norm_w), _col(kn_rope_w), wq_t, wkv_t,
        _col(qn_w), _col(qr_w), _col(kn_nope_w), shift, tm)

    ret_t = _retention(ret_decay_logit_fwd.astype(_F32), ret_decay_logit_bwd.astype(_F32),
                       rq_t, rk_t, rv_t, rg_t,
                       ret_gn_w.astype(_F32).reshape(RET_HEADS, RET_V, 1), chunk)

    att_t, ssq = _attention(fast, qt, k, vt, mg_t, _col(mla_out_norm_w), tq, tk)

    return _out_proj(x, ret_t, att_t, ssq, mod3, w_out_t, tm_out)
```

```python
import functools

import jax
import jax.numpy as jnp
from jax import lax
from jax.experimental import pallas as pl
from jax.experimental.pallas import tpu as pltpu

RET_HEADS = 8
RET_QK = 64
RET_V = 128
MLA_HEADS = 8
MLA_NOPE = 64
MLA_ROPE = 32
MLA_V = 128
Q_LORA = 384
KV_LORA = 256
ROPE_BASE = 10000.0
EPS = 1e-6
HEAD_PAD = 128
LOG2E = 1.4426950408889634

_SIZES = (RET_HEADS * RET_QK, RET_HEADS * RET_QK, RET_HEADS * RET_V, RET_HEADS * RET_V,
          Q_LORA, KV_LORA, MLA_ROPE, MLA_HEADS * MLA_V)
_OFFS = tuple(sum(_SIZES[:i]) for i in range(len(_SIZES) + 1))

VMEM_LIMIT = 56 * 1024 * 1024

_NT = (((1,), (1,)), ((), ()))
_TN = (((0,), (0,)), ((), ()))
_F32 = jnp.float32
_BF16 = jnp.bfloat16


def _silu(v):
    u = 0.5 * v
    return u + u * jnp.tanh(u)


def _params(sem):
    return pltpu.CompilerParams(dimension_semantics=sem, vmem_limit_bytes=VMEM_LIMIT)


def _ada_kernel(c_ref, w_ref, b_ref, o_ref):
    a = _silu(c_ref[...]).astype(_BF16)
    o_ref[...] = jnp.dot(a, w_ref[...].astype(_BF16), preferred_element_type=_F32) + b_ref[...]


def _ada(c, w_ada, b_ada):
    bsz, d = c.shape
    n = w_ada.shape[1]
    tn = 1024
    return pl.pallas_call(
        _ada_kernel,
        grid=(n // tn,),
        in_specs=[pl.BlockSpec((bsz, d), lambda j: (0, 0)),
                  pl.BlockSpec((d, tn), lambda j: (0, j)),
                  pl.BlockSpec((1, tn), lambda j: (0, j))],
        out_specs=pl.BlockSpec((bsz, tn), lambda j: (0, j)),
        out_shape=jax.ShapeDtypeStruct((bsz, n), _F32),
        compiler_params=_params(("arbitrary",)),
        name="ada_mod",
    )(c, w_ada, b_ada.reshape(1, n))


def _inproj_kernel(x_ref, pos_ref, mod_ref, nw_ref, w_ref, inv64_ref, inv32_ref,
                   qnw_ref, kvnw_ref, krw_ref, wq_ref, wkv_ref, qnn_ref, qrn_ref, knn_ref, shift_ref,
                   rq_ref, rk_ref, rv_ref, rg_ref, mg_ref,
                   qt_ref, k_ref, vt_ref):
    tm = x_ref.shape[1]
    x = x_ref[0]
    shift = mod_ref[0, 0:1, :]
    scale = mod_ref[0, 1:2, :]
    gain = nw_ref[...] * (1.0 + scale)
    h = (x * lax.rsqrt(jnp.mean(x * x, axis=-1, keepdims=True) + EPS) * gain + shift).astype(_BF16)

    def proj(i, j=None):
        return lax.dot_general(w_ref[_OFFS[i]:_OFFS[i + 1 if j is None else j], :], h, _NT,
                               preferred_element_type=_F32)

    def rms(v, w_col):
        return v * lax.rsqrt(jnp.mean(v * v, axis=0, keepdims=True) + EPS) * w_col

    def rot(v, cos, sin):
        n = v.shape[0] // 2
        return v[:n] * cos - v[n:] * sin, v[n:] * cos + v[:n] * sin

    pos = pos_ref[0].astype(_F32)
    ang64 = inv64_ref[...] * pos
    cos64, sin64 = jnp.cos(ang64), jnp.sin(ang64)
    ang32 = inv32_ref[...] * pos
    cos32, sin32 = jnp.cos(ang32), jnp.sin(ang32)

    lat = proj(4, 7)
    cq = rms(lat[:Q_LORA], qnw_ref[...]).astype(_BF16)
    ckv = rms(lat[Q_LORA:Q_LORA + KV_LORA], kvnw_ref[...]).astype(_BF16)
    kr = jnp.concatenate(rot(rms(lat[Q_LORA + KV_LORA:], krw_ref[...]), cos32, sin32), axis=0)
    q_all = jnp.dot(wq_ref[...], cq, preferred_element_type=_F32)
    kv_all = jnp.dot(wkv_ref[...], ckv, preferred_element_type=_F32)
    pad_row = lax.broadcasted_iota(jnp.int32, (HEAD_PAD - MLA_NOPE - MLA_ROPE, tm), 0)
    qpad = jnp.where(pad_row == 0, shift_ref[...], 0.0)
    kpad = jnp.where(pad_row == 0, 1.0, 0.0)
    hq = MLA_NOPE + MLA_ROPE
    hkv = MLA_NOPE + MLA_V
    for hd in range(MLA_HEADS):
        qn = rms(q_all[hd * hq: hd * hq + MLA_NOPE], qnn_ref[...])
        qr = rot(rms(q_all[hd * hq + MLA_NOPE: (hd + 1) * hq], qrn_ref[...]), cos32, sin32)
        qt_ref[0, hd] = jnp.concatenate([qn, qr[0], qr[1], qpad], axis=0).astype(_BF16)
        kn = rms(kv_all[hd * hkv: hd * hkv + MLA_NOPE], knn_ref[...])
        kh = jnp.concatenate([kn, kr, kpad], axis=0)
        k_ref[0, hd] = kh.T.astype(_BF16)
        vt_ref[0, hd] = kv_all[hd * hkv + MLA_NOPE: (hd + 1) * hkv].astype(_BF16)

    rg_ref[0] = _silu(proj(3)).astype(_BF16)
    mg_ref[0] = _silu(proj(7)).astype(_BF16)
    qk = proj(0, 2)
    k_mul = RET_QK ** -0.5
    for src, dst, cs, sn in ((0, rq_ref, cos64, sin64), (1, rk_ref, cos64 * k_mul, sin64 * k_mul)):
        for hd in range(RET_HEADS):
            lo = _OFFS[src] + hd * RET_QK
            r1, r2 = rot(qk[lo: lo + RET_QK], cs, sn)
            dst[0, hd * RET_QK: hd * RET_QK + RET_QK // 2, :] = r1.astype(_BF16)
            dst[0, hd * RET_QK + RET_QK // 2: (hd + 1) * RET_QK, :] = r2.astype(_BF16)
    rv_ref[0] = proj(2).astype(_BF16)


def _inproj(x, pos3, mod3, norm_w, w_in_t, inv64, inv32, qnw, kvnw, krw, wq_t, wkv_t, qnn, qrn, knn, shift, tm):
    bsz, s, d = x.shape
    const = lambda b, i: (0, 0)
    tok = lambda b, i: (b, 0, i)
    head_t = lambda b, i: (b, 0, 0, i)
    sizes = (_SIZES[0], _SIZES[1], _SIZES[2], _SIZES[3], _SIZES[7])
    out_shape = [jax.ShapeDtypeStruct((bsz, n, s), _BF16) for n in sizes]
    out_specs = [pl.BlockSpec((1, n, tm), tok) for n in sizes]
    out_shape += [jax.ShapeDtypeStruct((bsz, MLA_HEADS, HEAD_PAD, s), _BF16),
                  jax.ShapeDtypeStruct((bsz, MLA_HEADS, s, HEAD_PAD), _BF16),
                  jax.ShapeDtypeStruct((bsz, MLA_HEADS, MLA_V, s), _BF16)]
    out_specs += [pl.BlockSpec((1, MLA_HEADS, HEAD_PAD, tm), head_t),
                  pl.BlockSpec((1, MLA_HEADS, tm, HEAD_PAD), lambda b, i: (b, 0, i, 0)),
                  pl.BlockSpec((1, MLA_HEADS, MLA_V, tm), head_t)]
    consts = (w_in_t, inv64, inv32, qnw, kvnw, krw, wq_t, wkv_t, qnn, qrn, knn, shift)
    return pl.pallas_call(
        _inproj_kernel,
        grid=(bsz, s // tm),
        in_specs=[pl.BlockSpec((1, tm, d), lambda b, i: (b, i, 0)),
                  pl.BlockSpec((1, 1, tm), tok),
                  pl.BlockSpec((1, 3, d), lambda b, i: (b, 0, 0)),
                  pl.BlockSpec((1, d), const)] + [pl.BlockSpec(a.shape, const) for a in consts],
        out_specs=out_specs,
        out_shape=out_shape,
        compiler_params=_params(("parallel", "parallel")),
        name="in_proj",
    )(x, pos3, mod3, norm_w.reshape(1, d), *consts)


RET_UNROLL = 32


def _log_sigmoid(v):
    return jnp.minimum(v, 0.0) - jnp.log(1.0 + jnp.exp(-jnp.abs(v)))


def _retention_kernel(lf_ref, lb_ref, q_ref, k_ref, v_ref, g_ref, gnw_ref, o_ref,
                      kv_ref, st_ref, *, chunk, n_chunks):
    hd = pl.program_id(1)
    c = chunk
    lgf = _log_sigmoid(jnp.full((1, 1), lf_ref[hd], _F32))
    lgb = _log_sigmoid(jnp.full((1, 1), lb_ref[hd], _F32))

    pos_l = lax.broadcasted_iota(jnp.int32, (1, c), 1).astype(_F32)
    kdec_f = jnp.exp(lgf * (c - 1.0 - pos_l))
    kdec_b = jnp.exp(lgb * pos_l)
    qdec_f = jnp.exp(lgf * (pos_l + 1.0))
    qdec_b = jnp.exp(lgb * (c - pos_l))
    jj = lax.broadcasted_iota(jnp.int32, (c, c), 0).astype(_F32)
    ii = lax.broadcasted_iota(jnp.int32, (c, c), 1).astype(_F32)
    dmat = jnp.where(ii >= jj, jnp.exp(lgf * jnp.maximum(ii - jj, 0.0)),
                     jnp.exp(lgb * jnp.maximum(jj - ii, 0.0)))
    lane = lax.broadcasted_iota(jnp.int32, (1, 2 * RET_QK), 1)
    is_f = lane < RET_QK
    sdec = jnp.where(is_f, jnp.exp(lgf * c), jnp.exp(lgb * c))

    def chunk_ds(i):
        return pl.ds(pl.multiple_of(i * c, c), c)

    def kv_body(i, carry):
        ds = chunk_ds(i)
        kc = k_ref[0, :, ds].astype(_F32)
        kfb = jnp.concatenate([kc * kdec_f, kc * kdec_b], axis=0).astype(_BF16)
        kv_ref[i] = lax.dot_general(v_ref[0, :, ds], kfb, _NT, preferred_element_type=_F32)
        return carry

    unroll = min(n_chunks, RET_UNROLL)
    lax.fori_loop(0, n_chunks, kv_body, 0, unroll=unroll)

    def bwd_body(t, sb):
        i = n_chunks - 1 - t
        st_ref[i] = sb
        return sb * sdec + kv_ref[i]

    lax.fori_loop(0, n_chunks, bwd_body, jnp.zeros((RET_V, 2 * RET_QK), _F32))

    gnw = gnw_ref[0]

    def out_body(i, sf):
        ds = chunk_ds(i)
        qc = q_ref[0, :, ds]
        kc = k_ref[0, :, ds]
        st = lax.dot_general(kc, qc, _TN, preferred_element_type=_F32)
        pt = (st * dmat).astype(_BF16)
        qf = qc.astype(_F32)
        qfb = jnp.concatenate([qf * qdec_f, qf * qdec_b], axis=0).astype(_BF16)
        state = jnp.where(is_f, sf, st_ref[i]).astype(_BF16)
        o = (jnp.dot(v_ref[0, :, ds], pt, preferred_element_type=_F32)
             + jnp.dot(state, qfb, preferred_element_type=_F32))
        mu = jnp.mean(o, axis=0, keepdims=True)
        dlt = o - mu
        var = jnp.mean(dlt * dlt, axis=0, keepdims=True)
        on = dlt * lax.rsqrt(var + EPS) * gnw
        o_ref[0, :, ds] = on.astype(_BF16) * g_ref[0, :, ds]
        return sf * sdec + kv_ref[i]

    lax.fori_loop(0, n_chunks, out_body, jnp.zeros((RET_V, 2 * RET_QK), _F32), unroll=unroll)


def _retention(lf, lb, rq_t, rk_t, rv_t, rg_t, gnw3, chunk):
    bsz, _, s = rq_t.shape
    n_chunks = s // chunk
    smem = pl.BlockSpec(memory_space=pltpu.SMEM)
    return pl.pallas_call(
        functools.partial(_retention_kernel, chunk=chunk, n_chunks=n_chunks),
        grid=(bsz, RET_HEADS),
        in_specs=[smem, smem,
                  pl.BlockSpec((1, RET_QK, s), lambda b, h: (b, h, 0)),
                  pl.BlockSpec((1, RET_QK, s), lambda b, h: (b, h, 0)),
                  pl.BlockSpec((1, RET_V, s), lambda b, h: (b, h, 0)),
                  pl.BlockSpec((1, RET_V, s), lambda b, h: (b, h, 0)),
                  pl.BlockSpec((1, RET_V, 1), lambda b, h: (h, 0, 0))],
        out_specs=pl.BlockSpec((1, RET_V, s), lambda b, h: (b, h, 0)),
        out_shape=jax.ShapeDtypeStruct((bsz, RET_HEADS * RET_V, s), _BF16),
        scratch_shapes=[pltpu.VMEM((n_chunks, RET_V, 2 * RET_QK), _F32),
                        pltpu.VMEM((n_chunks, RET_V, 2 * RET_QK), _F32)],
        compiler_params=_params(("parallel", "parallel")),
        name="retention",
    )(lf, lb, rq_t, rk_t, rv_t, rg_t, gnw3)


def _attn_kernel(fast_ref, qt_ref, k_ref, vt_ref, o_ref, *bufs, tq, tk, tqf, tkf, n_q, n_kv):
    nb = len(bufs)
    fast = fast_ref[0]

    def q_ds(qi):
        return pl.ds(pl.multiple_of(qi * tq, tq), tq)

    def kv_ds(j):
        return pl.ds(pl.multiple_of(j * tk, tk), tk)

    def write(qi, l, acc):
        o_ref[0, :, q_ds(qi)] = (acc * (1.0 / l)).astype(_BF16)

    @pl.when(fast != 0)
    def _fast_path():
        def body(qi, carry):
            qds = pl.ds(pl.multiple_of(qi * tqf, tqf), tqf)
            qt = qt_ref[0, 0, :, qds]
            l = jnp.zeros((1, tqf), _F32)
            acc = jnp.zeros((MLA_V, tqf), _F32)
            for j in range(n_kv * tk // tkf):
                ds = pl.ds(j * tkf, tkf)
                st = jnp.dot(k_ref[0, 0, ds, :], qt, preferred_element_type=_F32)
                p = jnp.exp2(st)
                l = l + jnp.sum(p, axis=0, keepdims=True)
                acc = acc + jnp.dot(vt_ref[0, 0, :, ds], p.astype(_BF16),
                                    preferred_element_type=_F32)
            o_ref[0, :, qds] = (acc * (1.0 / l)).astype(_BF16)
            return carry

        lax.fori_loop(0, n_q * tq // tqf, body, 0, unroll=2)

    @pl.when(fast == 0)
    def _safe_path():
        def scores(qi, j, m, s_ref):
            st = jnp.dot(k_ref[0, 0, kv_ds(j), :], qt_ref[0, 0, :, q_ds(qi)],
                         preferred_element_type=_F32)
            s_ref[...] = st
            return jnp.maximum(m, jnp.max(st, axis=0, keepdims=True))

        def accumulate(j, m_prev, m, l, acc, s_ref):
            alpha = jnp.exp2(m_prev - m)
            p = jnp.exp2(s_ref[...] - m)
            l = alpha * l + jnp.sum(p, axis=0, keepdims=True)
            acc = alpha * acc + jnp.dot(vt_ref[0, 0, :, kv_ds(j)], p.astype(_BF16),
                                        preferred_element_type=_F32)
            return l, acc

        neg_inf = jnp.full((1, tq), -jnp.inf, _F32)

        def body(qi, carry):
            ms = [neg_inf] + list(carry)
            nxt = []
            qn = jnp.minimum(qi + 1, n_q - 1)
            l = jnp.zeros((1, tq), _F32)
            acc = jnp.zeros((MLA_V, tq), _F32)
            for j in range(n_kv):
                t = j + ATTN_AHEAD
                if t < n_kv:
                    ms.append(scores(qi, t, ms[-1], bufs[t % nb]))
                else:
                    nxt.append(scores(qn, t - n_kv, nxt[-1] if nxt else neg_inf, bufs[t % nb]))
                l, acc = accumulate(j, ms[j], ms[j + 1], l, acc, bufs[j % nb])
            write(qi, l, acc)
            return tuple(nxt)

        first = []
        for j in range(ATTN_AHEAD):
            first.append(scores(0, j, first[-1] if first else neg_inf, bufs[j]))
        lax.fori_loop(0, n_q, body, tuple(first))


ATTN_AHEAD = 2
ATTN_BUFS = 4
SCORE_BOUND_LOG2 = 60.0
ATTN_FAST_TK = 4096
ATTN_FAST_TQ = 1024


def _attention(fast, qt, k, vt, tq, tk):
    bsz, nh, _, s = qt.shape
    n_kv = s // tk
    assert s % tq == 0 and s % tk == 0 and n_kv % ATTN_BUFS == 0 and ATTN_AHEAD < ATTN_BUFS
    return pl.pallas_call(
        functools.partial(_attn_kernel, tq=tq, tk=tk, tqf=min(ATTN_FAST_TQ, s), tkf=min(ATTN_FAST_TK, s),
                          n_q=s // tq, n_kv=s // tk),
        grid=(bsz, nh),
        in_specs=[pl.BlockSpec(memory_space=pltpu.SMEM),
                  pl.BlockSpec((1, 1, HEAD_PAD, s), lambda b, h: (b, h, 0, 0)),
                  pl.BlockSpec((1, 1, s, HEAD_PAD), lambda b, h: (b, h, 0, 0)),
                  pl.BlockSpec((1, 1, MLA_V, s), lambda b, h: (b, h, 0, 0))],
        out_specs=pl.BlockSpec((1, MLA_V, s), lambda b, h: (b, h, 0)),
        out_shape=jax.ShapeDtypeStruct((bsz, nh * MLA_V, s), _BF16),
        scratch_shapes=[pltpu.VMEM((tk, tq), _F32)] * ATTN_BUFS,
        compiler_params=_params(("parallel", "parallel")),
        name="mla_attention",
    )(fast, qt, k, vt)


OUT_BLOCK = 256


def _out_kernel(x_ref, ret_ref, att_ref, mg_ref, mod_ref, onw_ref, w_ref, o_ref):
    att = att_ref[0].astype(_F32)
    mla = att * lax.rsqrt(jnp.mean(att * att, axis=0, keepdims=True) + EPS) * onw_ref[...]
    mla = (mla * mg_ref[0].astype(_F32)).astype(_BF16)
    cat = jnp.concatenate([ret_ref[0], mla], axis=0)
    d = w_ref.shape[0]
    for lo in range(0, d, OUT_BLOCK):
        yt = jnp.dot(w_ref[lo:lo + OUT_BLOCK, :], cat, preferred_element_type=_F32)
        gate = mod_ref[0, 2:3, lo:lo + OUT_BLOCK]
        o_ref[0, :, lo:lo + OUT_BLOCK] = x_ref[0, :, lo:lo + OUT_BLOCK] + gate * yt.T


def _out_proj(x, ret_t, att_t, mg_t, mod3, onw, w_out_t, tm):
    bsz, s, d = x.shape
    tok = lambda b, i: (b, 0, i)
    const = lambda b, i: (0, 0)
    return pl.pallas_call(
        _out_kernel,
        grid=(bsz, s // tm),
        in_specs=[pl.BlockSpec((1, tm, d), lambda b, i: (b, i, 0)),
                  pl.BlockSpec((1, ret_t.shape[1], tm), tok),
                  pl.BlockSpec((1, att_t.shape[1], tm), tok),
                  pl.BlockSpec((1, mg_t.shape[1], tm), tok),
                  pl.BlockSpec((1, 3, d), lambda b, i: (b, 0, 0)),
                  pl.BlockSpec(onw.shape, const),
                  pl.BlockSpec(w_out_t.shape, const)],
        out_specs=pl.BlockSpec((1, tm, d), lambda b, i: (b, i, 0)),
        out_shape=jax.ShapeDtypeStruct((bsz, s, d), x.dtype),
        compiler_params=_params(("parallel", "parallel")),
        name="out_proj",
    )(x, ret_t, att_t, mg_t, mod3, onw, w_out_t)


def _score_bound(qn_w, qr_w, kn_w, kr_w):
    def sq(w, d):
        return d * jnp.max(jnp.square(w.astype(_F32)))
    bound2 = (sq(qn_w, MLA_NOPE) + sq(qr_w, MLA_ROPE)) * (sq(kn_w, MLA_NOPE) + sq(kr_w, MLA_ROPE))
    ok = bound2 <= SCORE_BOUND_LOG2 ** 2
    shift = jnp.where(ok, -jnp.sqrt(bound2), 0.0).astype(_F32)
    return ok.astype(jnp.int32).reshape(1), shift.reshape(1, 1)


def _col(v):
    return v.astype(_F32).reshape(-1, 1)


def _block_sizes(s):
    tm = min(512, s)
    tm_out = min(1024, s)
    tq = min(512, s)
    tk = min(1024, s // ATTN_BUFS)
    chunk = min(256, s)
    return tm, tm_out, tq, tk, chunk


def kernel(x, c, positions, norm_w, w_ada, b_ada, w_in, ret_decay_logit_fwd, ret_decay_logit_bwd, ret_gn_w, q_norm_w, w_uq, kv_norm_w, w_ukv, qn_nope_w, qn_rope_w, kn_nope_w, kn_rope_w, mla_out_norm_w, w_out):
    bsz, s, d = x.shape
    tm, tm_out, tq, tk, chunk = _block_sizes(s)
    assert s % tm == 0 and s % tm_out == 0 and s % chunk == 0 and d == w_in.shape[0]

    w_in_t = w_in.T.astype(_BF16)
    wq_t = w_uq.T.astype(_BF16)
    wkv_t = w_ukv.T.astype(_BF16)
    w_out_t = w_out.T.astype(_BF16)
    inv64 = (ROPE_BASE ** (-jnp.arange(0, RET_QK, 2, dtype=_F32) / RET_QK)).reshape(-1, 1)
    inv32 = (ROPE_BASE ** (-jnp.arange(0, MLA_ROPE, 2, dtype=_F32) / MLA_ROPE)).reshape(-1, 1)

    mod3 = _ada(c, w_ada, b_ada).reshape(bsz, 3, d)
    pos3 = positions.reshape(bsz, 1, s)

    q_mul = (MLA_NOPE + MLA_ROPE) ** -0.5 * LOG2E
    qn_w = qn_nope_w.astype(_F32) * q_mul
    qr_w = qn_rope_w.astype(_F32) * q_mul
    fast, shift = _score_bound(qn_w, qr_w, kn_nope_w, kn_rope_w)

    (rq_t, rk_t, rv_t, rg_t, mg_t, qt, k, vt) = _inproj(
        x, pos3, mod3, norm_w, w_in_t, inv64, inv32,
        _col(q_norm_w), _col(kv_norm_w), _col(kn_rope_w), wq_t, wkv_t,
        _col(qn_w), _col(qr_w), _col(kn_nope_w), shift, tm)

    ret_t = _retention(ret_decay_logit_fwd.astype(_F32), ret_decay_logit_bwd.astype(_F32),
                       rq_t, rk_t, rv_t, rg_t,
                       ret_gn_w.astype(_F32).reshape(RET_HEADS, RET_V, 1), chunk)

    att_t = _attention(fast, qt, k, vt, tq, tk)

    return _out_proj(x, ret_t, att_t, mg_t, mod3, _col(mla_out_norm_w), w_out_t, tm_out)
```

```python
import functools

import jax
import jax.numpy as jnp
from jax import lax
from jax.experimental import pallas as pl
from jax.experimental.pallas import tpu as pltpu

RET_HEADS = 8
RET_QK = 64
RET_V = 128
MLA_HEADS = 8
MLA_NOPE = 64
MLA_ROPE = 32
MLA_V = 128
Q_LORA = 384
KV_LORA = 256
ROPE_BASE = 10000.0
EPS = 1e-6
HEAD_PAD = 128
LOG2E = 1.4426950408889634

_SIZES = (RET_HEADS * RET_QK, RET_HEADS * RET_QK, RET_HEADS * RET_V, RET_HEADS * RET_V,
          Q_LORA, KV_LORA, MLA_ROPE, MLA_HEADS * MLA_V)
_OFFS = tuple(sum(_SIZES[:i]) for i in range(len(_SIZES) + 1))

VMEM_LIMIT = 56 * 1024 * 1024

_NT = (((1,), (1,)), ((), ()))
_TN = (((0,), (0,)), ((), ()))
_F32 = jnp.float32
_BF16 = jnp.bfloat16


def _silu(v):
    u = 0.5 * v
    return u + u * jnp.tanh(u)


def _params(sem):
    return pltpu.CompilerParams(dimension_semantics=sem, vmem_limit_bytes=VMEM_LIMIT)


def _ada_kernel(c_ref, w_ref, b_ref, o_ref):
    a = _silu(c_ref[...]).astype(_BF16)
    o_ref[...] = jnp.dot(a, w_ref[...].astype(_BF16), preferred_element_type=_F32) + b_ref[...]


def _ada(c, w_ada, b_ada):
    bsz, d = c.shape
    n = w_ada.shape[1]
    tn = 1024
    return pl.pallas_call(
        _ada_kernel,
        grid=(n // tn,),
        in_specs=[pl.BlockSpec((bsz, d), lambda j: (0, 0)),
                  pl.BlockSpec((d, tn), lambda j: (0, j)),
                  pl.BlockSpec((1, tn), lambda j: (0, j))],
        out_specs=pl.BlockSpec((bsz, tn), lambda j: (0, j)),
        out_shape=jax.ShapeDtypeStruct((bsz, n), _F32),
        compiler_params=_params(("arbitrary",)),
        name="ada_mod",
    )(c, w_ada, b_ada.reshape(1, n))


def _inproj_kernel(x_ref, pos_ref, mod_ref, nw_ref, w_ref, inv64_ref, inv32_ref,
                   qnw_ref, kvnw_ref, krw_ref, wq_ref, wkv_ref, qnn_ref, qrn_ref, knn_ref, shift_ref,
                   rq_ref, rk_ref, rv_ref, rg_ref, mg_ref,
                   qt_ref, k_ref, vt_ref):
    tm = x_ref.shape[1]
    x = x_ref[0]
    shift = mod_ref[0, 0:1, :]
    scale = mod_ref[0, 1:2, :]
    gain = nw_ref[...] * (1.0 + scale)
    h = (x * lax.rsqrt(jnp.mean(x * x, axis=-1, keepdims=True) + EPS) * gain + shift).astype(_BF16)

    def proj(i, j=None):
        return lax.dot_general(w_ref[_OFFS[i]:_OFFS[i + 1 if j is None else j], :], h, _NT,
                               preferred_element_type=_F32)

    def rms(v, w_col):
        return v * lax.rsqrt(jnp.mean(v * v, axis=0, keepdims=True) + EPS) * w_col

    def rot(v, cos, sin):
        n = v.shape[0] // 2
        return v[:n] * cos - v[n:] * sin, v[n:] * cos + v[:n] * sin

    pos = pos_ref[0].astype(_F32)
    ang64 = inv64_ref[...] * pos
    cos64, sin64 = jnp.cos(ang64), jnp.sin(ang64)
    ang32 = inv32_ref[...] * pos
    cos32, sin32 = jnp.cos(ang32), jnp.sin(ang32)

    lat = proj(4, 7)
    cq = rms(lat[:Q_LORA], qnw_ref[...]).astype(_BF16)
    ckv = rms(lat[Q_LORA:Q_LORA + KV_LORA], kvnw_ref[...]).astype(_BF16)
    kr = jnp.concatenate(rot(rms(lat[Q_LORA + KV_LORA:], krw_ref[...]), cos32, sin32), axis=0)
    q_all = jnp.dot(wq_ref[...], cq, preferred_element_type=_F32)
    kv_all = jnp.dot(wkv_ref[...], ckv, preferred_element_type=_F32)
    pad_row = lax.broadcasted_iota(jnp.int32, (HEAD_PAD - MLA_NOPE - MLA_ROPE, tm), 0)
    qpad = jnp.where(pad_row == 0, shift_ref[...], 0.0)
    kpad = jnp.where(pad_row == 0, 1.0, 0.0)
    hq = MLA_NOPE + MLA_ROPE
    hkv = MLA_NOPE + MLA_V
    for hd in range(MLA_HEADS):
        qn = rms(q_all[hd * hq: hd * hq + MLA_NOPE], qnn_ref[...])
        qr = rot(rms(q_all[hd * hq + MLA_NOPE: (hd + 1) * hq], qrn_ref[...]), cos32, sin32)
        qt_ref[0, hd] = jnp.concatenate([qn, qr[0], qr[1], qpad], axis=0).astype(_BF16)
        kn = rms(kv_all[hd * hkv: hd * hkv + MLA_NOPE], knn_ref[...])
        kh = jnp.concatenate([kn, kr, kpad], axis=0)
        k_ref[0, hd] = kh.T.astype(_BF16)
        vt_ref[0, hd] = kv_all[hd * hkv + MLA_NOPE: (hd + 1) * hkv].astype(_BF16)

    rg_ref[0] = _silu(proj(3)).astype(_BF16)
    mg_ref[0] = _silu(proj(7)).astype(_BF16)
    qk = proj(0, 2)
    k_mul = RET_QK ** -0.5
    for src, dst, cs, sn in ((0, rq_ref, cos64, sin64), (1, rk_ref, cos64 * k_mul, sin64 * k_mul)):
        for hd in range(RET_HEADS):
            lo = _OFFS[src] + hd * RET_QK
            r1, r2 = rot(qk[lo: lo + RET_QK], cs, sn)
            dst[0, hd * RET_QK: hd * RET_QK + RET_QK // 2, :] = r1.astype(_BF16)
            dst[0, hd * RET_QK + RET_QK // 2: (hd + 1) * RET_QK, :] = r2.astype(_BF16)
    rv_ref[0] = proj(2).astype(_BF16)


def _inproj(x, pos3, mod3, norm_w, w_in_t, inv64, inv32, qnw, kvnw, krw, wq_t, wkv_t, qnn, qrn, knn, shift, tm):
    bsz, s, d = x.shape
    const = lambda b, i: (0, 0)
    tok = lambda b, i: (b, 0, i)
    head_t = lambda b, i: (b, 0, 0, i)
    sizes = (_SIZES[0], _SIZES[1], _SIZES[2], _SIZES[3], _SIZES[7])
    out_shape = [jax.ShapeDtypeStruct((bsz, n, s), _BF16) for n in sizes]
    out_specs = [pl.BlockSpec((1, n, tm), tok) for n in sizes]
    out_shape += [jax.ShapeDtypeStruct((bsz, MLA_HEADS, HEAD_PAD, s), _BF16),
                  jax.ShapeDtypeStruct((bsz, MLA_HEADS, s, HEAD_PAD), _BF16),
                  jax.ShapeDtypeStruct((bsz, MLA_HEADS, MLA_V, s), _BF16)]
    out_specs += [pl.BlockSpec((1, MLA_HEADS, HEAD_PAD, tm), head_t),
                  pl.BlockSpec((1, MLA_HEADS, tm, HEAD_PAD), lambda b, i: (b, 0, i, 0)),
                  pl.BlockSpec((1, MLA_HEADS, MLA_V, tm), head_t)]
    consts = (w_in_t, inv64, inv32, qnw, kvnw, krw, wq_t, wkv_t, qnn, qrn, knn, shift)
    return pl.pallas_call(
        _inproj_kernel,
        grid=(bsz, s // tm),
        in_specs=[pl.BlockSpec((1, tm, d), lambda b, i: (b, i, 0)),
                  pl.BlockSpec((1, 1, tm), tok),
                  pl.BlockSpec((1, 3, d), lambda b, i: (b, 0, 0)),
                  pl.BlockSpec((1, d), const)] + [pl.BlockSpec(a.shape, const) for a in consts],
        out_specs=out_specs,
        out_shape=out_shape,
        compiler_params=_params(("parallel", "parallel")),
        name="in_proj",
    )(x, pos3, mod3, norm_w.reshape(1, d), *consts)


RET_UNROLL = 32


def _log_sigmoid(v):
    return jnp.minimum(v, 0.0) - jnp.log(1.0 + jnp.exp(-jnp.abs(v)))


def _retention_kernel(lf_ref, lb_ref, q_ref, k_ref, v_ref, g_ref, gnw_ref, o_ref,
                      kv_ref, st_ref, *, chunk, n_chunks):
    hd = pl.program_id(1)
    c = chunk
    lgf = _log_sigmoid(jnp.full((1, 1), lf_ref[hd], _F32))
    lgb = _log_sigmoid(jnp.full((1, 1), lb_ref[hd], _F32))

    pos_l = lax.broadcasted_iota(jnp.int32, (1, c), 1).astype(_F32)
    kdec_f = jnp.exp(lgf * (c - 1.0 - pos_l))
    kdec_b = jnp.exp(lgb * pos_l)
    qdec_f = jnp.exp(lgf * (pos_l + 1.0))
    qdec_b = jnp.exp(lgb * (c - pos_l))
    jj = lax.broadcasted_iota(jnp.int32, (c, c), 0).astype(_F32)
    ii = lax.broadcasted_iota(jnp.int32, (c, c), 1).astype(_F32)
    dmat = jnp.where(ii >= jj, jnp.exp(lgf * jnp.maximum(ii - jj, 0.0)),
                     jnp.exp(lgb * jnp.maximum(jj - ii, 0.0)))
    lane = lax.broadcasted_iota(jnp.int32, (1, 2 * RET_QK), 1)
    is_f = lane < RET_QK
    sdec = jnp.where(is_f, jnp.exp(lgf * c), jnp.exp(lgb * c))

    def chunk_ds(i):
        return pl.ds(pl.multiple_of(i * c, c), c)

    def kv_body(i, carry):
        ds = chunk_ds(i)
        kc = k_ref[0, :, ds].astype(_F32)
        kfb = jnp.concatenate([kc * kdec_f, kc * kdec_b], axis=0).astype(_BF16)
        kv_ref[i] = lax.dot_general(v_ref[0, :, ds], kfb, _NT, preferred_element_type=_F32)
        return carry

    unroll = min(n_chunks, RET_UNROLL)
    lax.fori_loop(0, n_chunks, kv_body, 0, unroll=unroll)

    def bwd_body(t, sb):
        i = n_chunks - 1 - t
        st_ref[i] = sb
        return sb * sdec + kv_ref[i]

    lax.fori_loop(0, n_chunks, bwd_body, jnp.zeros((RET_V, 2 * RET_QK), _F32))

    gnw = gnw_ref[0]

    def out_body(i, sf):
        ds = chunk_ds(i)
        qc = q_ref[0, :, ds]
        kc = k_ref[0, :, ds]
        st = lax.dot_general(kc, qc, _TN, preferred_element_type=_F32)
        pt = (st * dmat).astype(_BF16)
        qf = qc.astype(_F32)
        qfb = jnp.concatenate([qf * qdec_f, qf * qdec_b], axis=0).astype(_BF16)
        state = jnp.where(is_f, sf, st_ref[i]).astype(_BF16)
        o = (jnp.dot(v_ref[0, :, ds], pt, preferred_element_type=_F32)
             + jnp.dot(state, qfb, preferred_element_type=_F32))
        mu = jnp.mean(o, axis=0, keepdims=True)
        dlt = o - mu
        var = jnp.mean(dlt * dlt, axis=0, keepdims=True)
        on = dlt * lax.rsqrt(var + EPS) * gnw
        o_ref[0, :, ds] = on.astype(_BF16) * g_ref[0, :, ds]
        return sf * sdec + kv_ref[i]

    lax.fori_loop(0, n_chunks, out_body, jnp.zeros((RET_V, 2 * RET_QK), _F32), unroll=unroll)


def _retention(lf, lb, rq_t, rk_t, rv_t, rg_t, gnw3, chunk):
    bsz, _, s = rq_t.shape
    n_chunks = s // chunk
    smem = pl.BlockSpec(memory_space=pltpu.SMEM)
    return pl.pallas_call(
        functools.partial(_retention_kernel, chunk=chunk, n_chunks=n_chunks),
        grid=(bsz, RET_HEADS),
        in_specs=[smem, smem,
                  pl.BlockSpec((1, RET_QK, s), lambda b, h: (b, h, 0)),
                  pl.BlockSpec((1, RET_QK, s), lambda b, h: (b, h, 0)),
                  pl.BlockSpec((1, RET_V, s), lambda b, h: (b, h, 0)),
                  pl.BlockSpec((1, RET_V, s), lambda b, h: (b, h, 0)),
                  pl.BlockSpec((1, RET_V, 1), lambda b, h: (h, 0, 0))],
        out_specs=pl.BlockSpec((1, RET_V, s), lambda b, h: (b, h, 0)),
        out_shape=jax.ShapeDtypeStruct((bsz, RET_HEADS * RET_V, s), _BF16),
        scratch_shapes=[pltpu.VMEM((n_chunks, RET_V, 2 * RET_QK), _F32),
                        pltpu.VMEM((n_chunks, RET_V, 2 * RET_QK), _F32)],
        compiler_params=_params(("parallel", "parallel")),
        name="retention",
    )(lf, lb, rq_t, rk_t, rv_t, rg_t, gnw3)


def _attn_kernel(fast_ref, qt_ref, k_ref, vt_ref, o_ref, *bufs, tq, tk, tqf, tkf, n_q, n_kv):
    nb = len(bufs)
    fast = fast_ref[0]

    def q_ds(qi):
        return pl.ds(pl.multiple_of(qi * tq, tq), tq)

    def kv_ds(j):
        return pl.ds(pl.multiple_of(j * tk, tk), tk)

    def write(qi, l, acc):
        o_ref[0, :, q_ds(qi)] = (acc * (1.0 / l)).astype(_BF16)

    @pl.when(fast != 0)
    def _fast_path():
        def body(qi, carry):
            qds = pl.ds(pl.multiple_of(qi * tqf, tqf), tqf)
            qt = qt_ref[0, 0, :, qds]
            l = jnp.zeros((1, tqf), _F32)
            acc = jnp.zeros((MLA_V, tqf), _F32)
            for j in range(n_kv * tk // tkf):
                ds = pl.ds(j * tkf, tkf)
                st = jnp.dot(k_ref[0, 0, ds, :], qt, preferred_element_type=_F32)
                p = jnp.exp2(st)
                l = l + jnp.sum(p, axis=0, keepdims=True)
                acc = acc + jnp.dot(vt_ref[0, 0, :, ds], p.astype(_BF16),
                                    preferred_element_type=_F32)
            o_ref[0, :, qds] = (acc * (1.0 / l)).astype(_BF16)
            return carry

        lax.fori_loop(0, n_q * tq // tqf, body, 0, unroll=4)

    @pl.when(fast == 0)
    def _safe_path():
        def scores(qi, j, m, s_ref):
            st = jnp.dot(k_ref[0, 0, kv_ds(j), :], qt_ref[0, 0, :, q_ds(qi)],
                         preferred_element_type=_F32)
            s_ref[...] = st
            return jnp.maximum(m, jnp.max(st, axis=0, keepdims=True))

        def accumulate(j, m_prev, m, l, acc, s_ref):
            alpha = jnp.exp2(m_prev - m)
            p = jnp.exp2(s_ref[...] - m)
            l = alpha * l + jnp.sum(p, axis=0, keepdims=True)
            acc = alpha * acc + jnp.dot(vt_ref[0, 0, :, kv_ds(j)], p.astype(_BF16),
                                        preferred_element_type=_F32)
            return l, acc

        neg_inf = jnp.full((1, tq), -jnp.inf, _F32)

        def body(qi, carry):
            ms = [neg_inf] + list(carry)
            nxt = []
            qn = jnp.minimum(qi + 1, n_q - 1)
            l = jnp.zeros((1, tq), _F32)
            acc = jnp.zeros((MLA_V, tq), _F32)
            for j in range(n_kv):
                t = j + ATTN_AHEAD
                if t < n_kv:
                    ms.append(scores(qi, t, ms[-1], bufs[t % nb]))
                else:
                    nxt.append(scores(qn, t - n_kv, nxt[-1] if nxt else neg_inf, bufs[t % nb]))
                l, acc = accumulate(j, ms[j], ms[j + 1], l, acc, bufs[j % nb])
            write(qi, l, acc)
            return tuple(nxt)

        first = []
        for j in range(ATTN_AHEAD):
            first.append(scores(0, j, first[-1] if first else neg_inf, bufs[j]))
        lax.fori_loop(0, n_q, body, tuple(first))


ATTN_AHEAD = 2
ATTN_BUFS = 4
SCORE_BOUND_LOG2 = 60.0
ATTN_FAST_TK = 4096
ATTN_FAST_TQ = 1024


def _attention(fast, qt, k, vt, tq, tk):
    bsz, nh, _, s = qt.shape
    n_kv = s // tk
    assert s % tq == 0 and s % tk == 0 and n_kv % ATTN_BUFS == 0 and ATTN_AHEAD < ATTN_BUFS
    return pl.pallas_call(
        functools.partial(_attn_kernel, tq=tq, tk=tk, tqf=min(ATTN_FAST_TQ, s), tkf=min(ATTN_FAST_TK, s),
                          n_q=s // tq, n_kv=s // tk),
        grid=(bsz, nh),
        in_specs=[pl.BlockSpec(memory_space=pltpu.SMEM),
                  pl.BlockSpec((1, 1, HEAD_PAD, s), lambda b, h: (b, h, 0, 0)),
                  pl.BlockSpec((1, 1, s, HEAD_PAD), lambda b, h: (b, h, 0, 0)),
                  pl.BlockSpec((1, 1, MLA_V, s), lambda b, h: (b, h, 0, 0))],
        out_specs=pl.BlockSpec((1, MLA_V, s), lambda b, h: (b, h, 0)),
        out_shape=jax.ShapeDtypeStruct((bsz, nh * MLA_V, s), _BF16),
        scratch_shapes=[pltpu.VMEM((tk, tq), _F32)] * ATTN_BUFS,
        compiler_params=_params(("parallel", "parallel")),
        name="mla_attention",
    )(fast, qt, k, vt)


OUT_BLOCK = 256


def _out_kernel(x_ref, ret_ref, att_ref, mg_ref, mod_ref, onw_ref, w_ref, o_ref):
    att = att_ref[0].astype(_F32)
    mla = att * lax.rsqrt(jnp.mean(att * att, axis=0, keepdims=True) + EPS) * onw_ref[...]
    mla = (mla * mg_ref[0].astype(_F32)).astype(_BF16)
    cat = jnp.concatenate([ret_ref[0], mla], axis=0)
    d = w_ref.shape[0]
    for lo in range(0, d, OUT_BLOCK):
        yt = jnp.dot(w_ref[lo:lo + OUT_BLOCK, :], cat, preferred_element_type=_F32)
        gate = mod_ref[0, 2:3, lo:lo + OUT_BLOCK]
        o_ref[0, :, lo:lo + OUT_BLOCK] = x_ref[0, :, lo:lo + OUT_BLOCK] + gate * yt.T


def _out_proj(x, ret_t, att_t, mg_t, mod3, onw, w_out_t, tm):
    bsz, s, d = x.shape
    tok = lambda b, i: (b, 0, i)
    const = lambda b, i: (0, 0)
    return pl.pallas_call(
        _out_kernel,
        grid=(bsz, s // tm),
        in_specs=[pl.BlockSpec((1, tm, d), lambda b, i: (b, i, 0)),
                  pl.BlockSpec((1, ret_t.shape[1], tm), tok),
                  pl.BlockSpec((1, att_t.shape[1], tm), tok),
                  pl.BlockSpec((1, mg_t.shape[1], tm), tok),
                  pl.BlockSpec((1, 3, d), lambda b, i: (b, 0, 0)),
                  pl.BlockSpec(onw.shape, const),
                  pl.BlockSpec(w_out_t.shape, const)],
        out_specs=pl.BlockSpec((1, tm, d), lambda b, i: (b, i, 0)),
        out_shape=jax.ShapeDtypeStruct((bsz, s, d), x.dtype),
        compiler_params=_params(("parallel", "parallel")),
        name="out_proj",
    )(x, ret_t, att_t, mg_t, mod3, onw, w_out_t)


def _score_bound(qn_w, qr_w, kn_w, kr_w):
    def sq(w, d):
        return d * jnp.max(jnp.square(w.astype(_F32)))
    bound2 = (sq(qn_w, MLA_NOPE) + sq(qr_w, MLA_ROPE)) * (sq(kn_w, MLA_NOPE) + sq(kr_w, MLA_ROPE))
    ok = bound2 <= SCORE_BOUND_LOG2 ** 2
    shift = jnp.where(ok, -jnp.sqrt(bound2), 0.0).astype(_F32)
    return ok.astype(jnp.int32).reshape(1), shift.reshape(1, 1)


def _col(v):
    return v.astype(_F32).reshape(-1, 1)


def _block_sizes(s):
    tm = min(512, s)
    tm_out = min(1024, s)
    tq = min(512, s)
    tk = min(1024, s // ATTN_BUFS)
    chunk = min(256, s)
    return tm, tm_out, tq, tk, chunk


def kernel(x, c, positions, norm_w, w_ada, b_ada, w_in, ret_decay_logit_fwd, ret_decay_logit_bwd, ret_gn_w, q_norm_w, w_uq, kv_norm_w, w_ukv, qn_nope_w, qn_rope_w, kn_nope_w, kn_rope_w, mla_out_norm_w, w_out):
    bsz, s, d = x.shape
    tm, tm_out, tq, tk, chunk = _block_sizes(s)
    assert s % tm == 0 and s % tm_out == 0 and s % chunk == 0 and d == w_in.shape[0]

    w_in_t = w_in.T.astype(_BF16)
    wq_t = w_uq.T.astype(_BF16)
    wkv_t = w_ukv.T.astype(_BF16)
    w_out_t = w_out.T.astype(_BF16)
    inv64 = (ROPE_BASE ** (-jnp.arange(0, RET_QK, 2, dtype=_F32) / RET_QK)).reshape(-1, 1)
    inv32 = (ROPE_BASE ** (-jnp.arange(0, MLA_ROPE, 2, dtype=_F32) / MLA_ROPE)).reshape(-1, 1)

    mod3 = _ada(c, w_ada, b_ada).reshape(bsz, 3, d)
    pos3 = positions.reshape(bsz, 1, s)

    q_mul = (MLA_NOPE + MLA_ROPE) ** -0.5 * LOG2E
    qn_w = qn_nope_w.astype(_F32) * q_mul
    qr_w = qn_rope_w.astype(_F32) * q_mul
    fast, shift = _score_bound(qn_w, qr_w, kn_nope_w, kn_rope_w)

    (rq_t, rk_t, rv_t, rg_t, mg_t, qt, k, vt) = _inproj(
        x, pos3, mod3, norm_w, w_in_t, inv64, inv32,
        _col(q_norm_w), _col(kv_norm_w), _col(kn_rope_w), wq_t, wkv_t,
        _col(qn_w), _col(qr_w), _col(kn_nope_w), shift, tm)

    ret_t = _retention(ret_decay_logit_fwd.astype(_F32), ret_decay_logit_bwd.astype(_F32),
                       rq_t, rk_t, rv_t, rg_t,
                       ret_gn_w.astype(_F32).reshape(RET_HEADS, RET_V, 1), chunk)

    att_t = _attention(fast, qt, k, vt, tq, tk)

    return _out_proj(x, ret_t, att_t, mg_t, mod3, _col(mla_out_norm_w), w_out_t, tm_out)
```

```python
import functools

import jax
import jax.numpy as jnp
from jax import lax
from jax.experimental import pallas as pl
from jax.experimental.pallas import tpu as pltpu

RET_HEADS = 8
RET_QK = 64
RET_V = 128
MLA_HEADS = 8
MLA_NOPE = 64
MLA_ROPE = 32
MLA_V = 128
Q_LORA = 384
KV_LORA = 256
ROPE_BASE = 10000.0
EPS = 1e-6
HEAD_PAD = 128
LOG2E = 1.4426950408889634

_SIZES = (RET_HEADS * RET_QK, RET_HEADS * RET_QK, RET_HEADS * RET_V, RET_HEADS * RET_V,
          Q_LORA, KV_LORA, MLA_ROPE, MLA_HEADS * MLA_V)
_OFFS = tuple(sum(_SIZES[:i]) for i in range(len(_SIZES) + 1))

VMEM_LIMIT = 56 * 1024 * 1024

_NT = (((1,), (1,)), ((), ()))
_TN = (((0,), (0,)), ((), ()))
_F32 = jnp.float32
_BF16 = jnp.bfloat16


def _silu(v):
    u = 0.5 * v
    return u + u * jnp.tanh(u)


def _params(sem):
    return pltpu.CompilerParams(dimension_semantics=sem, vmem_limit_bytes=VMEM_LIMIT)


def _ada_kernel(c_ref, w_ref, b_ref, o_ref):
    a = _silu(c_ref[...]).astype(_BF16)
    o_ref[...] = jnp.dot(a, w_ref[...].astype(_BF16), preferred_element_type=_F32) + b_ref[...]


def _ada(c, w_ada, b_ada):
    bsz, d = c.shape
    n = w_ada.shape[1]
    tn = 1024
    return pl.pallas_call(
        _ada_kernel,
        grid=(n // tn,),
        in_specs=[pl.BlockSpec((bsz, d), lambda j: (0, 0)),
                  pl.BlockSpec((d, tn), lambda j: (0, j)),
                  pl.BlockSpec((1, tn), lambda j: (0, j))],
        out_specs=pl.BlockSpec((bsz, tn), lambda j: (0, j)),
        out_shape=jax.ShapeDtypeStruct((bsz, n), _F32),
        compiler_params=_params(("arbitrary",)),
        name="ada_mod",
    )(c, w_ada, b_ada.reshape(1, n))


def _inproj_kernel(x_ref, pos_ref, mod_ref, nw_ref, w_ref, inv64_ref, inv32_ref,
                   qnw_ref, kvnw_ref, krw_ref, wq_ref, wkv_ref, qnn_ref, qrn_ref, knn_ref, shift_ref,
                   rq_ref, rk_ref, rv_ref, rg_ref, mg_ref,
                   qt_ref, k_ref, vt_ref):
    tm = x_ref.shape[1]
    x = x_ref[0]
    shift = mod_ref[0, 0:1, :]
    scale = mod_ref[0, 1:2, :]
    gain = nw_ref[...] * (1.0 + scale)
    h = (x * lax.rsqrt(jnp.mean(x * x, axis=-1, keepdims=True) + EPS) * gain + shift).astype(_BF16)

    def proj(i, j=None):
        return lax.dot_general(w_ref[_OFFS[i]:_OFFS[i + 1 if j is None else j], :], h, _NT,
                               preferred_element_type=_F32)

    def rms(v, w_col):
        return v * lax.rsqrt(jnp.mean(v * v, axis=0, keepdims=True) + EPS) * w_col

    def rot(v, cos, sin):
        n = v.shape[0] // 2
        return v[:n] * cos - v[n:] * sin, v[n:] * cos + v[:n] * sin

    pos = pos_ref[0].astype(_F32)
    ang64 = inv64_ref[...] * pos
    cos64, sin64 = jnp.cos(ang64), jnp.sin(ang64)
    ang32 = inv32_ref[...] * pos
    cos32, sin32 = jnp.cos(ang32), jnp.sin(ang32)

    lat = proj(4, 7)
    cq = rms(lat[:Q_LORA], qnw_ref[...]).astype(_BF16)
    ckv = rms(lat[Q_LORA:Q_LORA + KV_LORA], kvnw_ref[...]).astype(_BF16)
    kr = jnp.concatenate(rot(rms(lat[Q_LORA + KV_LORA:], krw_ref[...]), cos32, sin32), axis=0)
    q_all = jnp.dot(wq_ref[...], cq, preferred_element_type=_F32)
    kv_all = jnp.dot(wkv_ref[...], ckv, preferred_element_type=_F32)
    pad_row = lax.broadcasted_iota(jnp.int32, (HEAD_PAD - MLA_NOPE - MLA_ROPE, tm), 0)
    qpad = jnp.where(pad_row == 0, shift_ref[...], 0.0)
    kpad = jnp.where(pad_row == 0, 1.0, 0.0)
    hq = MLA_NOPE + MLA_ROPE
    hkv = MLA_NOPE + MLA_V
    for hd in range(MLA_HEADS):
        qn = rms(q_all[hd * hq: hd * hq + MLA_NOPE], qnn_ref[...])
        qr = rot(rms(q_all[hd * hq + MLA_NOPE: (hd + 1) * hq], qrn_ref[...]), cos32, sin32)
        qt_ref[0, hd] = jnp.concatenate([qn, qr[0], qr[1], qpad], axis=0).astype(_BF16)
        kn = rms(kv_all[hd * hkv: hd * hkv + MLA_NOPE], knn_ref[...])
        kh = jnp.concatenate([kn, kr, kpad], axis=0)
        k_ref[0, hd] = kh.T.astype(_BF16)
        vt_ref[0, hd] = kv_all[hd * hkv + MLA_NOPE: (hd + 1) * hkv].astype(_BF16)

    rg_ref[0] = _silu(proj(3)).astype(_BF16)
    mg_ref[0] = _silu(proj(7)).astype(_BF16)
    qk = proj(0, 2)
    k_mul = RET_QK ** -0.5
    for src, dst, cs, sn in ((0, rq_ref, cos64, sin64), (1, rk_ref, cos64 * k_mul, sin64 * k_mul)):
        for hd in range(RET_HEADS):
            lo = _OFFS[src] + hd * RET_QK
            r1, r2 = rot(qk[lo: lo + RET_QK], cs, sn)
            dst[0, hd * RET_QK: hd * RET_QK + RET_QK // 2, :] = r1.astype(_BF16)
            dst[0, hd * RET_QK + RET_QK // 2: (hd + 1) * RET_QK, :] = r2.astype(_BF16)
    rv_ref[0] = proj(2).astype(_BF16)


def _inproj(x, pos3, mod3, norm_w, w_in_t, inv64, inv32, qnw, kvnw, krw, wq_t, wkv_t, qnn, qrn, knn, shift, tm):
    bsz, s, d = x.shape
    const = lambda b, i: (0, 0)
    tok = lambda b, i: (b, 0, i)
    head_t = lambda b, i: (b, 0, 0, i)
    sizes = (_SIZES[0], _SIZES[1], _SIZES[2], _SIZES[3], _SIZES[7])
    out_shape = [jax.ShapeDtypeStruct((bsz, n, s), _BF16) for n in sizes]
    out_specs = [pl.BlockSpec((1, n, tm), tok) for n in sizes]
    out_shape += [jax.ShapeDtypeStruct((bsz, MLA_HEADS, HEAD_PAD, s), _BF16),
                  jax.ShapeDtypeStruct((bsz, MLA_HEADS, s, HEAD_PAD), _BF16),
                  jax.ShapeDtypeStruct((bsz, MLA_HEADS, MLA_V, s), _BF16)]
    out_specs += [pl.BlockSpec((1, MLA_HEADS, HEAD_PAD, tm), head_t),
                  pl.BlockSpec((1, MLA_HEADS, tm, HEAD_PAD), lambda b, i: (b, 0, i, 0)),
                  pl.BlockSpec((1, MLA_HEADS, MLA_V, tm), head_t)]
    consts = (w_in_t, inv64, inv32, qnw, kvnw, krw, wq_t, wkv_t, qnn, qrn, knn, shift)
    return pl.pallas_call(
        _inproj_kernel,
        grid=(bsz, s // tm),
        in_specs=[pl.BlockSpec((1, tm, d), lambda b, i: (b, i, 0)),
                  pl.BlockSpec((1, 1, tm), tok),
                  pl.BlockSpec((1, 3, d), lambda b, i: (b, 0, 0)),
                  pl.BlockSpec((1, d), const)] + [pl.BlockSpec(a.shape, const) for a in consts],
        out_specs=out_specs,
        out_shape=out_shape,
        compiler_params=_params(("parallel", "parallel")),
        name="in_proj",
    )(x, pos3, mod3, norm_w.reshape(1, d), *consts)


RET_UNROLL = 32


def _log_sigmoid(v):
    return jnp.minimum(v, 0.0) - jnp.log(1.0 + jnp.exp(-jnp.abs(v)))


def _retention_kernel(lf_ref, lb_ref, q_ref, k_ref, v_ref, g_ref, gnw_ref, o_ref,
                      kv_ref, st_ref, *, chunk, n_chunks):
    hd = pl.program_id(1)
    c = chunk
    lgf = _log_sigmoid(jnp.full((1, 1), lf_ref[hd], _F32))
    lgb = _log_sigmoid(jnp.full((1, 1), lb_ref[hd], _F32))

    pos_l = lax.broadcasted_iota(jnp.int32, (1, c), 1).astype(_F32)
    kdec_f = jnp.exp(lgf * (c - 1.0 - pos_l))
    kdec_b = jnp.exp(lgb * pos_l)
    qdec_f = jnp.exp(lgf * (pos_l + 1.0))
    qdec_b = jnp.exp(lgb * (c - pos_l))
    jj = lax.broadcasted_iota(jnp.int32, (c, c), 0).astype(_F32)
    ii = lax.broadcasted_iota(jnp.int32, (c, c), 1).astype(_F32)
    dmat = jnp.where(ii >= jj, jnp.exp(lgf * jnp.maximum(ii - jj, 0.0)),
                     jnp.exp(lgb * jnp.maximum(jj - ii, 0.0)))
    lane = lax.broadcasted_iota(jnp.int32, (1, 2 * RET_QK), 1)
    is_f = lane < RET_QK
    sdec = jnp.where(is_f, jnp.exp(lgf * c), jnp.exp(lgb * c))

    def chunk_ds(i):
        return pl.ds(pl.multiple_of(i * c, c), c)

    def kv_body(i, carry):
        ds = chunk_ds(i)
        kc = k_ref[0, :, ds].astype(_F32)
        kfb = jnp.concatenate([kc * kdec_f, kc * kdec_b], axis=0).astype(_BF16)
        kv_ref[i] = lax.dot_general(v_ref[0, :, ds], kfb, _NT, preferred_element_type=_F32)
        return carry

    unroll = min(n_chunks, RET_UNROLL)
    lax.fori_loop(0, n_chunks, kv_body, 0, unroll=unroll)

    def bwd_body(t, sb):
        i = n_chunks - 1 - t
        st_ref[i] = sb
        return sb * sdec + kv_ref[i]

    lax.fori_loop(0, n_chunks, bwd_body, jnp.zeros((RET_V, 2 * RET_QK), _F32))

    gnw = gnw_ref[0]

    def out_body(i, sf):
        ds = chunk_ds(i)
        qc = q_ref[0, :, ds]
        kc = k_ref[0, :, ds]
        st = lax.dot_general(kc, qc, _TN, preferred_element_type=_F32)
        pt = (st * dmat).astype(_BF16)
        qf = qc.astype(_F32)
        qfb = jnp.concatenate([qf * qdec_f, qf * qdec_b], axis=0).astype(_BF16)
        state = jnp.where(is_f, sf, st_ref[i]).astype(_BF16)
        o = (jnp.dot(v_ref[0, :, ds], pt, preferred_element_type=_F32)
             + jnp.dot(state, qfb, preferred_element_type=_F32))
        mu = jnp.mean(o, axis=0, keepdims=True)
        dlt = o - mu
        var = jnp.mean(dlt * dlt, axis=0, keepdims=True)
        on = dlt * lax.rsqrt(var + EPS) * gnw
        o_ref[0, :, ds] = on.astype(_BF16) * g_ref[0, :, ds]
        return sf * sdec + kv_ref[i]

    lax.fori_loop(0, n_chunks, out_body, jnp.zeros((RET_V, 2 * RET_QK), _F32), unroll=unroll)


def _retention(lf, lb, rq_t, rk_t, rv_t, rg_t, gnw3, chunk):
    bsz, _, s = rq_t.shape
    n_chunks = s // chunk
    smem = pl.BlockSpec(memory_space=pltpu.SMEM)
    return pl.pallas_call(
        functools.partial(_retention_kernel, chunk=chunk, n_chunks=n_chunks),
        grid=(bsz, RET_HEADS),
        in_specs=[smem, smem,
                  pl.BlockSpec((1, RET_QK, s), lambda b, h: (b, h, 0)),
                  pl.BlockSpec((1, RET_QK, s), lambda b, h: (b, h, 0)),
                  pl.BlockSpec((1, RET_V, s), lambda b, h: (b, h, 0)),
                  pl.BlockSpec((1, RET_V, s), lambda b, h: (b, h, 0)),
                  pl.BlockSpec((1, RET_V, 1), lambda b, h: (h, 0, 0))],
        out_specs=pl.BlockSpec((1, RET_V, s), lambda b, h: (b, h, 0)),
        out_shape=jax.ShapeDtypeStruct((bsz, RET_HEADS * RET_V, s), _BF16),
        scratch_shapes=[pltpu.VMEM((n_chunks, RET_V, 2 * RET_QK), _F32),
                        pltpu.VMEM((n_chunks, RET_V, 2 * RET_QK), _F32)],
        compiler_params=_params(("parallel", "parallel")),
        name="retention",
    )(lf, lb, rq_t, rk_t, rv_t, rg_t, gnw3)


def _attn_kernel(fast_ref, qt_ref, k_ref, vt_ref, o_ref, *bufs, tq, tk, tqf, tkf, n_q, n_kv):
    nb = len(bufs)
    fast = fast_ref[0]

    def q_ds(qi):
        return pl.ds(pl.multiple_of(qi * tq, tq), tq)

    def kv_ds(j):
        return pl.ds(pl.multiple_of(j * tk, tk), tk)

    def write(qi, l, acc):
        o_ref[0, :, q_ds(qi)] = (acc * (1.0 / l)).astype(_BF16)

    @pl.when(fast != 0)
    def _fast_path():
        def body(qi, carry):
            qds = pl.ds(pl.multiple_of(qi * tqf, tqf), tqf)
            qt = qt_ref[0, 0, :, qds]
            l = jnp.zeros((1, tqf), _F32)
            acc = jnp.zeros((MLA_V, tqf), _F32)
            for j in range(n_kv * tk // tkf):
                ds = pl.ds(j * tkf, tkf)
                st = jnp.dot(k_ref[0, 0, ds, :], qt, preferred_element_type=_F32)
                p = jnp.exp2(st)
                l = l + jnp.sum(p, axis=0, keepdims=True)
                acc = acc + jnp.dot(vt_ref[0, 0, :, ds], p.astype(_BF16),
                                    preferred_element_type=_F32)
            o_ref[0, :, qds] = (acc * (1.0 / l)).astype(_BF16)
            return carry

        lax.fori_loop(0, n_q * tq // tqf, body, 0, unroll=2)

    @pl.when(fast == 0)
    def _safe_path():
        def scores(qi, j, m, s_ref):
            st = jnp.dot(k_ref[0, 0, kv_ds(j), :], qt_ref[0, 0, :, q_ds(qi)],
                         preferred_element_type=_F32)
            s_ref[...] = st
            return jnp.maximum(m, jnp.max(st, axis=0, keepdims=True))

        def accumulate(j, m_prev, m, l, acc, s_ref):
            alpha = jnp.exp2(m_prev - m)
            p = jnp.exp2(s_ref[...] - m)
            l = alpha * l + jnp.sum(p, axis=0, keepdims=True)
            acc = alpha * acc + jnp.dot(vt_ref[0, 0, :, kv_ds(j)], p.astype(_BF16),
                                        preferred_element_type=_F32)
            return l, acc

        neg_inf = jnp.full((1, tq), -jnp.inf, _F32)

        def body(qi, carry):
            ms = [neg_inf] + list(carry)
            nxt = []
            qn = jnp.minimum(qi + 1, n_q - 1)
            l = jnp.zeros((1, tq), _F32)
            acc = jnp.zeros((MLA_V, tq), _F32)
            for j in range(n_kv):
                t = j + ATTN_AHEAD
                if t < n_kv:
                    ms.append(scores(qi, t, ms[-1], bufs[t % nb]))
                else:
                    nxt.append(scores(qn, t - n_kv, nxt[-1] if nxt else neg_inf, bufs[t % nb]))
                l, acc = accumulate(j, ms[j], ms[j + 1], l, acc, bufs[j % nb])
            write(qi, l, acc)
            return tuple(nxt)

        first = []
        for j in range(ATTN_AHEAD):
            first.append(scores(0, j, first[-1] if first else neg_inf, bufs[j]))
        lax.fori_loop(0, n_q, body, tuple(first))


ATTN_AHEAD = 2
ATTN_BUFS = 4
SCORE_BOUND_LOG2 = 60.0
ATTN_FAST_TK = 4096
ATTN_FAST_TQ = 1024


def _attention(fast, qt, k, vt, tq, tk):
    bsz, nh, _, s = qt.shape
    n_kv = s // tk
    assert s % tq == 0 and s % tk == 0 and n_kv % ATTN_BUFS == 0 and ATTN_AHEAD < ATTN_BUFS
    return pl.pallas_call(
        functools.partial(_attn_kernel, tq=tq, tk=tk, tqf=min(ATTN_FAST_TQ, s), tkf=min(ATTN_FAST_TK, s),
                          n_q=s // tq, n_kv=s // tk),
        grid=(bsz, nh),
        in_specs=[pl.BlockSpec(memory_space=pltpu.SMEM),
                  pl.BlockSpec((1, 1, HEAD_PAD, s), lambda b, h: (b, h, 0, 0)),
                  pl.BlockSpec((1, 1, s, HEAD_PAD), lambda b, h: (b, h, 0, 0)),
                  pl.BlockSpec((1, 1, MLA_V, s), lambda b, h: (b, h, 0, 0))],
        out_specs=pl.BlockSpec((1, MLA_V, s), lambda b, h: (b, h, 0)),
        out_shape=jax.ShapeDtypeStruct((bsz, nh * MLA_V, s), _BF16),
        scratch_shapes=[pltpu.VMEM((tk, tq), _F32)] * ATTN_BUFS,
        compiler_params=_params(("parallel", "parallel")),
        name="mla_attention",
    )(fast, qt, k, vt)


OUT_BLOCK = 256


def _out_kernel(x_ref, ret_ref, att_ref, mg_ref, mod_ref, onw_ref, w_ref, o_ref):
    att = att_ref[0].astype(_F32)
    mla = att * lax.rsqrt(jnp.mean(att * att, axis=0, keepdims=True) + EPS) * onw_ref[...]
    mla = (mla * mg_ref[0].astype(_F32)).astype(_BF16)
    cat = jnp.concatenate([ret_ref[0], mla], axis=0)
    d = w_ref.shape[0]
    for lo in range(0, d, OUT_BLOCK):
        yt = jnp.dot(w_ref[lo:lo + OUT_BLOCK, :], cat, preferred_element_type=_F32)
        gate = mod_ref[0, 2:3, lo:lo + OUT_BLOCK]
        o_ref[0, :, lo:lo + OUT_BLOCK] = x_ref[0, :, lo:lo + OUT_BLOCK] + gate * yt.T


def _out_proj(x, ret_t, att_t, mg_t, mod3, onw, w_out_t, tm):
    bsz, s, d = x.shape
    tok = lambda b, i: (b, 0, i)
    const = lambda b, i: (0, 0)
    deep = pl.Buffered(3)
    stream_specs = [pl.BlockSpec((1, tm, d), lambda b, i: (b, i, 0), pipeline_mode=deep),
                    pl.BlockSpec((1, ret_t.shape[1], tm), tok, pipeline_mode=deep),
                    pl.BlockSpec((1, att_t.shape[1], tm), tok, pipeline_mode=deep),
                    pl.BlockSpec((1, mg_t.shape[1], tm), tok, pipeline_mode=deep),
                    pl.BlockSpec((1, 3, d), lambda b, i: (b, 0, 0))]
    out_spec = pl.BlockSpec((1, tm, d), lambda b, i: (b, i, 0))

    def outer(x_hbm, ret_hbm, att_hbm, mg_hbm, mod_hbm, onw_ref, w_ref, o_hbm):
        def body(x_ref, ret_ref, att_ref, mg_ref, mod_ref, o_ref):
            _out_kernel(x_ref, ret_ref, att_ref, mg_ref, mod_ref, onw_ref, w_ref, o_ref)

        pltpu.emit_pipeline(body, grid=(bsz, s // tm), in_specs=stream_specs, out_specs=[out_spec])(
            x_hbm, ret_hbm, att_hbm, mg_hbm, mod_hbm, o_hbm)

    any_spec = pl.BlockSpec(memory_space=pl.ANY)
    vmem_spec = pl.BlockSpec(memory_space=pltpu.VMEM)
    return pl.pallas_call(
        outer,
        in_specs=[any_spec] * 5 + [vmem_spec, vmem_spec],
        out_specs=any_spec,
        out_shape=jax.ShapeDtypeStruct((bsz, s, d), x.dtype),
        compiler_params=pltpu.CompilerParams(vmem_limit_bytes=VMEM_LIMIT),
        name="out_proj",
    )(x, ret_t, att_t, mg_t, mod3, onw, w_out_t)


def _score_bound(qn_w, qr_w, kn_w, kr_w):
    def sq(w, d):
        return d * jnp.max(jnp.square(w.astype(_F32)))
    bound2 = (sq(qn_w, MLA_NOPE) + sq(qr_w, MLA_ROPE)) * (sq(kn_w, MLA_NOPE) + sq(kr_w, MLA_ROPE))
    ok = bound2 <= SCORE_BOUND_LOG2 ** 2
    shift = jnp.where(ok, -jnp.sqrt(bound2), 0.0).astype(_F32)
    return ok.astype(jnp.int32).reshape(1), shift.reshape(1, 1)


def _col(v):
    return v.astype(_F32).reshape(-1, 1)


def _block_sizes(s):
    tm = min(512, s)
    tm_out = min(1024, s)
    tq = min(512, s)
    tk = min(1024, s // ATTN_BUFS)
    chunk = min(256, s)
    return tm, tm_out, tq, tk, chunk


def kernel(x, c, positions, norm_w, w_ada, b_ada, w_in, ret_decay_logit_fwd, ret_decay_logit_bwd, ret_gn_w, q_norm_w, w_uq, kv_norm_w, w_ukv, qn_nope_w, qn_rope_w, kn_nope_w, kn_rope_w, mla_out_norm_w, w_out):
    bsz, s, d = x.shape
    tm, tm_out, tq, tk, chunk = _block_sizes(s)
    assert s % tm == 0 and s % tm_out == 0 and s % chunk == 0 and d == w_in.shape[0]

    w_in_t = w_in.T.astype(_BF16)
    wq_t = w_uq.T.astype(_BF16)
    wkv_t = w_ukv.T.astype(_BF16)
    w_out_t = w_out.T.astype(_BF16)
    inv64 = (ROPE_BASE ** (-jnp.arange(0, RET_QK, 2, dtype=_F32) / RET_QK)).reshape(-1, 1)
    inv32 = (ROPE_BASE ** (-jnp.arange(0, MLA_ROPE, 2, dtype=_F32) / MLA_ROPE)).reshape(-1, 1)

    mod3 = _ada(c, w_ada, b_ada).reshape(bsz, 3, d)
    pos3 = positions.reshape(bsz, 1, s)

    q_mul = (MLA_NOPE + MLA_ROPE) ** -0.5 * LOG2E
    qn_w = qn_nope_w.astype(_F32) * q_mul
    qr_w = qn_rope_w.astype(_F32) * q_mul
    fast, shift = _score_bound(qn_w, qr_w, kn_nope_w, kn_rope_w)

    (rq_t, rk_t, rv_t, rg_t, mg_t, qt, k, vt) = _inproj(
        x, pos3, mod3, norm_w, w_in_t, inv64, inv32,
        _col(q_norm_w), _col(kv_norm_w), _col(kn_rope_w), wq_t, wkv_t,
        _col(qn_w), _col(qr_w), _col(kn_nope_w), shift, tm)

    ret_t = _retention(ret_decay_logit_fwd.astype(_F32), ret_decay_logit_bwd.astype(_F32),
                       rq_t, rk_t, rv_t, rg_t,
                       ret_gn_w.astype(_F32).reshape(RET_HEADS, RET_V, 1), chunk)

    att_t = _attention(fast, qt, k, vt, tq, tk)

    return _out_proj(x, ret_t, att_t, mg_t, mod3, _col(mla_out_norm_w), w_out_t, tm_out)
```
